```python
import math
import jax, jax.numpy as jnp
from jax import lax
import numpy as np

D_MODEL = 1024
BATCH = 8
SEQ = 4096
DEPTH = 2
DEC_BATCH = 32
DEC_SEQ = 8
PAST_LEN = 16384
PAGE_SIZE = 128

N_META = 16
D_FF = 2816
CHUNK = 64
N_EVEN = (DEPTH + 1) // 2
N_ODD = DEPTH // 2
GLA_HEADS = 4
GLA_DK = 128
GLA_DV = 256
GLA_RANK = 16
GLA_GATE_NORM = 16.0
GLA_QK = GLA_HEADS * GLA_DK
GLA_V = GLA_HEADS * GLA_DV
SSD_HEADS = 16
SSD_HEADDIM = 64
SSD_STATE = 64
SSD_GROUPS = 2
SSD_CONV = 4
SSD_INNER = SSD_HEADS * SSD_HEADDIM
CONV_DIM = SSD_INNER + 2 * SSD_GROUPS * SSD_STATE
AB_SPLIT_SIZES = (GLA_QK, GLA_QK, GLA_V, GLA_RANK, GLA_V, SSD_INNER, CONV_DIM, SSD_HEADS)
IN_AB = 2 * GLA_QK + 2 * GLA_V + GLA_RANK + SSD_INNER + CONV_DIM + SSD_HEADS
MIX_AB = GLA_V + SSD_INNER
DA_HEADS = 8
DA_DH = 64
DA_VD = 2 * DA_DH
Q_BLOCK = 128

kernel_name = 'hybrid_gla_ssd_diffattn_macaron_step'


def rmsnorm(x, w, eps=1e-6):
    xf = x.astype(jnp.float32)
    y = xf * lax.rsqrt(jnp.mean(xf * xf, axis=-1, keepdims=True) + eps)
    return (y * w.astype(jnp.float32)).astype(x.dtype)


def swiglu_half(x, g_norm, w_gu, w_down):
    gate, up = jnp.split(rmsnorm(x, g_norm) @ w_gu, 2, axis=-1)
    return x + 0.5 * ((jax.nn.silu(gate) * up) @ w_down).astype(x.dtype)


def causal_dwconv(xpad, w, b):
    k_w = w.shape[0]
    L = xpad.shape[1] - (k_w - 1)
    y = b + xpad[:, 0:L] * w[0]
    for j in range(1, k_w):
        y = y + xpad[:, j:j + L] * w[j]
    return y


def gla_chunked(q, k, v, log_f, s0, c):
    bsz, L, H, DK = q.shape
    DV = v.shape[-1]
    n = L // c
    q = q.reshape(bsz, n, c, H, DK)
    k = k.reshape(bsz, n, c, H, DK)
    v = v.reshape(bsz, n, c, H, DV)
    b = jnp.cumsum(log_f.reshape(bsz, n, c, H, DK), axis=2)
    b_last = b[:, :, -1:]
    q_dec = q * jnp.exp(b)
    k_inv = k * jnp.exp(-b)
    k_end = k * jnp.exp(b_last - b)
    causal = jnp.tril(jnp.ones((c, c), bool))
    att = jnp.where(causal, jnp.einsum('bnthk,bnshk->bnhts', q_dec, k_inv), 0.0)
    o_intra = jnp.einsum('bnhts,bnshv->bnthv', att, v)
    ds = jnp.einsum('bnshk,bnshv->bnhkv', k_end, v)
    g = jnp.exp(b_last[:, :, 0])

    def step(s, inp):
        g_n, ds_n = inp
        return g_n[..., None] * s + ds_n, s

    s_fin, s_start = lax.scan(step, s0.astype(jnp.float32), (jnp.moveaxis(g, 1, 0), jnp.moveaxis(ds, 1, 0)))
    o_inter = jnp.einsum('bnthk,nbhkv->bnthv', q_dec, s_start)
    return (o_intra + o_inter).reshape(bsz, L, H, DV), s_fin


def ssd_chunked(x, da, bm, cm, s0, c):
    bsz, L, H, P = x.shape
    N = bm.shape[-1]
    n = L // c
    x = x.reshape(bsz, n, c, H, P)
    bm = bm.reshape(bsz, n, c, H, N)
    cm = cm.reshape(bsz, n, c, H, N)
    cs = jnp.cumsum(da.reshape(bsz, n, c, H), axis=2)
    causal = jnp.tril(jnp.ones((c, c), bool))[:, :, None]
    seg = cs[:, :, :, None, :] - cs[:, :, None, :, :]
    decay = jnp.exp(jnp.where(causal, seg, -jnp.inf))
    scores = jnp.einsum('bnthd,bnshd->bntsh', cm, bm) * decay
    y_intra = jnp.einsum('bntsh,bnshp->bnthp', scores, x)
    cs_last = cs[:, :, -1:]
    ds = jnp.einsum('bnsh,bnshd,bnshp->bnhpd', jnp.exp(cs_last - cs), bm, x)
    g = jnp.exp(cs_last[:, :, 0])

    def step(s, inp):
        g_n, ds_n = inp
        return g_n[..., None, None] * s + ds_n, s

    s_fin, s_start = lax.scan(step, s0.astype(jnp.float32), (jnp.moveaxis(g, 1, 0), jnp.moveaxis(ds, 1, 0)))
    y_inter = jnp.einsum('bnthd,nbhpd->bnthp', cm, s_start) * jnp.exp(cs)[..., None]
    return (y_intra + y_inter).reshape(bsz, L, H, P), s_fin


def mixer_ab(h, s_gla, s_ssd, conv_buf, segments, w_in, gk_w2, gk_b, gla_norm,
             conv_w, conv_b, dt_bias, a_log, d_skip, ssd_norm, w_out):
    bsz, L, _ = h.shape
    split_at = [int(i) for i in np.cumsum(AB_SPLIT_SIZES)[:-1]]
    q, k, v, gk_lo, g, z, xbc, dt = jnp.split(h @ w_in, split_at, axis=-1)
    q = q.reshape(bsz, L, GLA_HEADS, GLA_DK) * (GLA_DK ** -0.5)
    k = k.reshape(bsz, L, GLA_HEADS, GLA_DK)
    v = v.reshape(bsz, L, GLA_HEADS, GLA_DV)
    log_f = (jax.nn.log_sigmoid((gk_lo @ gk_w2 + gk_b).astype(jnp.float32)) / GLA_GATE_NORM
             ).reshape(bsz, L, GLA_HEADS, GLA_DK)
    xpad = jnp.concatenate([conv_buf.astype(xbc.dtype), xbc], axis=1)
    new_conv = xpad[:, xpad.shape[1] - (SSD_CONV - 1):]
    xbc = jax.nn.silu(causal_dwconv(xpad, conv_w, conv_b))
    xs, bm, cm = jnp.split(xbc, [SSD_INNER, SSD_INNER + SSD_GROUPS * SSD_STATE], axis=-1)
    xs = xs.reshape(bsz, L, SSD_HEADS, SSD_HEADDIM)
    rep = SSD_HEADS // SSD_GROUPS
    bm = jnp.repeat(bm.reshape(bsz, L, SSD_GROUPS, SSD_STATE), rep, axis=2)
    cm = jnp.repeat(cm.reshape(bsz, L, SSD_GROUPS, SSD_STATE), rep, axis=2)
    dt = jax.nn.softplus(dt.astype(jnp.float32) + dt_bias)
    da = dt * (-jnp.exp(a_log.astype(jnp.float32)))
    xdt = xs * dt[..., None]
    o_parts, y_parts = [], []
    start = 0
    for seg_len, chunk in segments:
        sl = slice(start, start + seg_len)
        o_seg, s_gla = gla_chunked(q[:, sl], k[:, sl], v[:, sl], log_f[:, sl], s_gla, chunk)
        y_seg, s_ssd = ssd_chunked(xdt[:, sl], da[:, sl], bm[:, sl], cm[:, sl], s_ssd, chunk)
        o_parts.append(o_seg)
        y_parts.append(y_seg)
        start += seg_len
    o = jnp.concatenate(o_parts, axis=1)
    y = jnp.concatenate(y_parts, axis=1)
    o = rmsnorm(o, gla_norm) * jax.nn.silu(g.reshape(bsz, L, GLA_HEADS, GLA_DV).astype(jnp.float32))
    y = (y + xs * d_skip[:, None]).reshape(bsz, L, SSD_INNER) * jax.nn.silu(z.astype(jnp.float32))
    y = rmsnorm(y.reshape(bsz, L, SSD_GROUPS, SSD_INNER // SSD_GROUPS), ssd_norm.reshape(SSD_GROUPS, -1))
    mix = jnp.concatenate([o.reshape(bsz, L, GLA_V), y.reshape(bsz, L, SSD_INNER)], axis=-1)
    return mix.astype(h.dtype) @ w_out, s_gla, s_ssd, new_conv


def alibi_slopes():
    return jnp.exp2(-8.0 * jnp.arange(1, DA_HEADS + 1, dtype=jnp.float32) / DA_HEADS)


def diff_lambda(lq1, lk1, lq2, lk2, lam_init):
    f = jnp.float32
    return (jnp.exp(jnp.sum(lq1.astype(f) * lk1.astype(f)))
            - jnp.exp(jnp.sum(lq2.astype(f) * lk2.astype(f))) + lam_init)


def diff_qkv(h, w_qkv):
    bsz, L, _ = h.shape
    q, k, v = jnp.split(h @ w_qkv, 3, axis=-1)
    return (q.reshape(bsz, L, DA_HEADS, 2, DA_DH), k.reshape(bsz, L, DA_HEADS, 2, DA_DH),
            v.reshape(bsz, L, DA_HEADS, DA_VD))


def diff_attn_core(q, k, v, q_pos, k_pos, lam, slopes):
    s = jnp.einsum('bqhcd,bkhcd->bchqk', q, k).astype(jnp.float32) * (DA_DH ** -0.5)
    dist = q_pos[:, None] - k_pos[None, :]
    bias = -slopes[:, None, None] * dist.astype(jnp.float32)
    s = jnp.where(dist >= 0, s + bias, -jnp.inf)
    p = jax.nn.softmax(s, axis=-1)
    a = p[:, 0] - lam * p[:, 1]
    return jnp.einsum('bhqk,bkhe->bqhe', a, v.astype(jnp.float32))


def diff_out(o, subln, w_o, lam_init, dtype):
    bsz, L = o.shape[0], o.shape[1]
    o = rmsnorm(o, subln, eps=1e-5) * (1.0 - lam_init)
    return o.reshape(bsz, L, DA_HEADS * DA_VD).astype(dtype) @ w_o


def diff_attn_prompt(h, w_qkv, lam, lam_init, subln, w_o):
    bsz, T, _ = h.shape
    q, k, v = diff_qkv(h, w_qkv)
    slopes = alibi_slopes()
    n_blk = -(-T // Q_BLOCK)
    qp = jnp.pad(q, ((0, 0), (0, n_blk * Q_BLOCK - T), (0, 0), (0, 0), (0, 0)))
    k_pos = jnp.arange(T)

    def block(i):
        q_blk = lax.dynamic_slice_in_dim(qp, i * Q_BLOCK, Q_BLOCK, axis=1)
        q_pos = i * Q_BLOCK + jnp.arange(Q_BLOCK)
        return diff_attn_core(q_blk, k, v, q_pos, k_pos, lam, slopes)

    o = lax.map(block, jnp.arange(n_blk))
    o = jnp.moveaxis(o, 0, 1).reshape(bsz, n_blk * Q_BLOCK, DA_HEADS, DA_VD)[:, :T]
    return diff_out(o, subln, w_o, lam_init, h.dtype), k.reshape(bsz, T, DA_HEADS, DA_VD), v


def diff_attn_sample(h, cache_k, cache_v, layer, page_table, w_qkv, lam, lam_init, subln, w_o):
    bsz, L, _ = h.shape
    q, k, v = diff_qkv(h, w_qkv)
    k_rows = k.reshape(bsz, L, DA_HEADS, DA_VD)
    past = page_table.shape[1] * PAGE_SIZE
    slopes = alibi_slopes()
    q_pos = past + jnp.arange(L)
    k_pos = jnp.arange(past + L)

    def one_seq(args):
        q_i, k_i, v_i, pages = args
        k_all = jnp.concatenate([cache_k[layer, pages].reshape(past, DA_HEADS, DA_VD).astype(k_i.dtype), k_i], axis=0)
        v_all = jnp.concatenate([cache_v[layer, pages].reshape(past, DA_HEADS, DA_VD).astype(v_i.dtype), v_i], axis=0)
        o_i = diff_attn_core(q_i[None], k_all.reshape(1, past + L, DA_HEADS, 2, DA_DH), v_all[None],
                             q_pos, k_pos, lam, slopes)
        return o_i[0]

    o = lax.map(one_seq, (q, k_rows, v, page_table))
    return diff_out(o, subln, w_o, lam_init, h.dtype), k_rows, v


def setup_inputs(seed: int = 0) -> dict:
    key = jax.random.key(seed)
    ks = jax.random.split(key, 48)
    counter = [0]
    f32 = jnp.float32

    def nk():
        counter[0] += 1
        return ks[counter[0] - 1]

    def nrm(shape, scale):
        return jax.random.normal(nk(), shape, f32) * scale

    def gain(shape):
        return 1.0 + 0.02 * jax.random.normal(nk(), shape, f32)

    n_pages = PAST_LEN // PAGE_SIZE
    n_pool = (5 * DEC_BATCH * n_pages) // 4
    page_table = jax.random.permutation(nk(), n_pool)[: DEC_BATCH * n_pages].reshape(DEC_BATCH, n_pages).astype(jnp.int32)
    dt0 = jnp.exp(jax.random.uniform(nk(), (N_EVEN, SSD_HEADS), f32, math.log(1e-3), math.log(1e-1)))
    dt_bias = dt0 + jnp.log(-jnp.expm1(-dt0))
    a_log = jnp.log(jax.random.uniform(nk(), (N_EVEN, SSD_HEADS), f32, 1.0, 16.0))
    return {
        'x_prompt': nrm((BATCH, SEQ, D_MODEL), 1.0),
        'x_sample': nrm((DEC_BATCH, DEC_SEQ, D_MODEL), 1.0),
        'state_gla': nrm((N_EVEN, DEC_BATCH, GLA_HEADS, GLA_DK, GLA_DV), 1.0),
        'state_ssd': nrm((N_EVEN, DEC_BATCH, SSD_HEADS, SSD_HEADDIM, SSD_STATE), 0.1),
        'state_conv': nrm((N_EVEN, DEC_BATCH, SSD_CONV - 1, CONV_DIM), 1.0),
        'cache_k': nrm((N_ODD, n_pool, PAGE_SIZE, DA_HEADS, DA_VD), 1.0),
        'cache_v': nrm((N_ODD, n_pool, PAGE_SIZE, DA_HEADS, DA_VD), 1.0),
        'page_table': page_table,
        'meta_tokens': nrm((N_META, D_MODEL), 1.0),
        'norm_ffn1': gain((DEPTH, D_MODEL)),
        'w_ffn1_gu': nrm((DEPTH, D_MODEL, 2 * D_FF), D_MODEL ** -0.5),
        'w_ffn1_down': nrm((DEPTH, D_FF, D_MODEL), D_FF ** -0.5),
        'norm_mix': gain((DEPTH, D_MODEL)),
        'norm_ffn2': gain((DEPTH, D_MODEL)),
        'w_ffn2_gu': nrm((DEPTH, D_MODEL, 2 * D_FF), D_MODEL ** -0.5),
        'w_ffn2_down': nrm((DEPTH, D_FF, D_MODEL), D_FF ** -0.5),
        'w_in_ab': nrm((N_EVEN, D_MODEL, IN_AB), D_MODEL ** -0.5),
        'gla_gk_w2': nrm((N_EVEN, GLA_RANK, GLA_QK), GLA_RANK ** -0.5),
        'gla_gk_b': nrm((N_EVEN, GLA_QK), 0.1),
        'gla_norm': gain((N_EVEN, GLA_DV)),
        'ssd_conv_w': nrm((N_EVEN, SSD_CONV, CONV_DIM), SSD_CONV ** -0.5),
        'ssd_conv_b': nrm((N_EVEN, CONV_DIM), 0.02),
        'ssd_dt_bias': dt_bias,
        'ssd_a_log': a_log,
        'ssd_d': gain((N_EVEN, SSD_HEADS)),
        'ssd_norm': gain((N_EVEN, SSD_INNER)),
        'w_out_ab': nrm((N_EVEN, MIX_AB, D_MODEL), MIX_AB ** -0.5),
        'w_qkv_c': nrm((N_ODD, D_MODEL, 3 * DA_HEADS * DA_VD), D_MODEL ** -0.5),
        'lam_q1': nrm((N_ODD, DA_DH), 0.1),
        'lam_k1': nrm((N_ODD, DA_DH), 0.1),
        'lam_q2': nrm((N_ODD, DA_DH), 0.1),
        'lam_k2': nrm((N_ODD, DA_DH), 0.1),
        'subln_c': gain((N_ODD, DA_VD)),
        'w_out_c': nrm((N_ODD, DA_HEADS * DA_VD, D_MODEL), (DA_HEADS * DA_VD) ** -0.5),
        'norm_final': gain((D_MODEL,)),
    }


def reference(x_prompt, x_sample, state_gla, state_ssd, state_conv, cache_k, cache_v, page_table,
              meta_tokens, norm_ffn1, w_ffn1_gu, w_ffn1_down, norm_mix, norm_ffn2, w_ffn2_gu, w_ffn2_down,
              w_in_ab, gla_gk_w2, gla_gk_b, gla_norm, ssd_conv_w, ssd_conv_b, ssd_dt_bias, ssd_a_log,
              ssd_d, ssd_norm, w_out_ab, w_qkv_c, lam_q1, lam_k1, lam_q2, lam_k2, subln_c, w_out_c,
              norm_final):
    bsz, seq, _ = x_prompt.shape
    dec_seq = x_sample.shape[1]
    meta = jnp.broadcast_to(meta_tokens.astype(x_prompt.dtype)[None], (bsz, N_META, D_MODEL))
    xp = jnp.concatenate([meta, x_prompt], axis=1)
    xs = x_sample
    prompt_segments = ((N_META, N_META), (seq, CHUNK))
    sample_segments = ((dec_seq, dec_seq),)
    gla_p, gla_s, ssd_p, ssd_s, conv_p, conv_s = [], [], [], [], [], []
    k_p, v_p, k_s, v_s = [], [], [], []
    for l in range(DEPTH):
        xp = swiglu_half(xp, norm_ffn1[l], w_ffn1_gu[l], w_ffn1_down[l])
        xs = swiglu_half(xs, norm_ffn1[l], w_ffn1_gu[l], w_ffn1_down[l])
        hp = rmsnorm(xp, norm_mix[l])
        hs = rmsnorm(xs, norm_mix[l])
        if l % 2 == 0:
            e = l // 2
            wts = (w_in_ab[e], gla_gk_w2[e], gla_gk_b[e], gla_norm[e], ssd_conv_w[e], ssd_conv_b[e],
                   ssd_dt_bias[e], ssd_a_log[e], ssd_d[e], ssd_norm[e], w_out_ab[e])
            z_gla = jnp.zeros((bsz, GLA_HEADS, GLA_DK, GLA_DV), jnp.float32)
            z_ssd = jnp.zeros((bsz, SSD_HEADS, SSD_HEADDIM, SSD_STATE), jnp.float32)
            z_conv = jnp.zeros((bsz, SSD_CONV - 1, CONV_DIM), hp.dtype)
            mp, sg, ss, cv = mixer_ab(hp, z_gla, z_ssd, z_conv, prompt_segments, *wts)
            gla_p.append(sg)
            ssd_p.append(ss)
            conv_p.append(cv)
            ms, sg, ss, cv = mixer_ab(hs, state_gla[e], state_ssd[e], state_conv[e], sample_segments, *wts)
            gla_s.append(sg)
            ssd_s.append(ss)
            conv_s.append(cv)
        else:
            o = l // 2
            lam_init = 0.8 - 0.6 * math.exp(-0.3 * l)
            lam = diff_lambda(lam_q1[o], lam_k1[o], lam_q2[o], lam_k2[o], lam_init)
            mp, kr, vr = diff_attn_prompt(hp, w_qkv_c[o], lam, lam_init, subln_c[o], w_out_c[o])
            k_p.append(kr)
            v_p.append(vr)
            ms, kr, vr = diff_attn_sample(hs, cache_k, cache_v, o, page_table, w_qkv_c[o], lam, lam_init,
                                          subln_c[o], w_out_c[o])
            k_s.append(kr)
            v_s.append(vr)
        xp = xp + mp.astype(xp.dtype)
        xs = xs + ms.astype(xs.dtype)
        xp = swiglu_half(xp, norm_ffn2[l], w_ffn2_gu[l], w_ffn2_down[l])
        xs = swiglu_half(xs, norm_ffn2[l], w_ffn2_gu[l], w_ffn2_down[l])
    y_prompt = rmsnorm(xp, norm_final)[:, N_META:]
    y_sample = rmsnorm(xs, norm_final)
    return (y_prompt, y_sample, jnp.stack(gla_p), jnp.stack(gla_s), jnp.stack(ssd_p), jnp.stack(ssd_s),
            jnp.stack(conv_p), jnp.stack(conv_s), jnp.stack(k_p), jnp.stack(v_p), jnp.stack(k_s), jnp.stack(v_s))
```

```python
import functools
import math

import jax
import jax.numpy as jnp
from jax import lax
from jax.experimental import pallas as pl
from jax.experimental.pallas import tpu as pltpu

F32 = jnp.float32
BF16 = jnp.bfloat16

CHUNK = 64
GLA_HEADS, GLA_DK, GLA_DV, GLA_RANK = 4, 128, 256, 16
GLA_GATE_NORM = 16.0
SSD_HEADS, SSD_HEADDIM, SSD_STATE, SSD_GROUPS, SSD_CONV = 16, 64, 64, 2, 4
SSD_INNER = SSD_HEADS * SSD_HEADDIM
DA_HEADS, DA_DH = 8, 64
DA_VD = 2 * DA_DH
PAGE = 128

LANES = 128
SUBLANES = 8
VMEM_LIMIT = 56 * 1024 * 1024

NEG_INF = float("-inf")


def _cparams(n_axes):
    return pltpu.CompilerParams(dimension_semantics=("arbitrary",) * n_axes,
                                vmem_limit_bytes=VMEM_LIMIT)


def _row_tile(n, target):
    best = SUBLANES
    for t in range(SUBLANES, target + 1, SUBLANES):
        if n % t == 0:
            best = t
    return best


def _resident(shape):
    nd = len(shape)
    return pl.BlockSpec(shape, lambda *_: (0,) * nd, pipeline_mode=pl.Buffered(1))


def _rms(x, w, eps):
    ms = jnp.mean(x * x, axis=-1, keepdims=True)
    return x * lax.rsqrt(ms + eps) * w


def _silu(x):
    return x * jax.nn.sigmoid(x)


def _softplus(x):
    return jnp.maximum(x, 0.0) + jnp.log1p(jnp.exp(-jnp.abs(x)))


def _dot(a, b):
    return jnp.dot(a, b, preferred_element_type=F32)


def _dot_nt(a, b):
    return lax.dot_general(a, b, (((1,), (1,)), ((), ())), preferred_element_type=F32)


def _dot_tn(a, b):
    return lax.dot_general(a, b, (((0,), (0,)), ((), ())), preferred_element_type=F32)


def _split3(a):
    hi = a.astype(BF16)
    r1 = a - hi.astype(F32)
    mid = r1.astype(BF16)
    lo = (r1 - mid.astype(F32)).astype(BF16)
    return hi, mid, lo


def _dot01_left(m01, a):
    hi, mid, lo = _split3(a)
    return _dot(m01, hi) + _dot(m01, mid) + _dot(m01, lo)


def _dot01_right(a, m01):
    hi, mid, lo = _split3(a)
    return _dot(hi, m01) + _dot(mid, m01) + _dot(lo, m01)


FFN_CHUNK = 256


def _ffn_body(x_ref, nw_ref, wg_ref, wu_ref, wd_ref, fw_ref, o_ref, acc_ref, xn_ref, *,
              n_chunks, final_norm):
    x = x_ref[...]
    xn_ref[...] = _rms(x, nw_ref[...], 1e-6).astype(BF16)
    acc_ref[...] = jnp.zeros_like(acc_ref)

    def chunk(c, carry):
        xn = xn_ref[...]
        g = _dot(xn, wg_ref[c])
        u = _dot(xn, wu_ref[c])
        a = (_silu(g) * u).astype(BF16)
        acc_ref[...] += _dot(a, wd_ref[c])
        return carry

    lax.fori_loop(0, n_chunks, chunk, 0)
    y = x_ref[...] + 0.5 * acc_ref[...]
    if final_norm:
        y = _rms(y, fw_ref[...], 1e-6)
    o_ref[...] = y


def _ffn(x, nw, wg, wu, wd, fw, final_norm):
    n, d = x.shape
    n_chunks = wg.shape[0]
    tm = _row_tile(n, 896)
    body = functools.partial(_ffn_body, n_chunks=n_chunks, final_norm=final_norm)
    return pl.pallas_call(
        body,
        out_shape=jax.ShapeDtypeStruct((n, d), F32),
        grid=(n // tm,),
        in_specs=[pl.BlockSpec((tm, d), lambda i: (i, 0)),
                  _resident(nw.shape), _resident(wg.shape), _resident(wu.shape),
                  _resident(wd.shape), _resident(fw.shape)],
        out_specs=pl.BlockSpec((tm, d), lambda i: (i, 0)),
        scratch_shapes=[pltpu.VMEM((tm, d), F32), pltpu.VMEM((tm, d), BF16)],
        compiler_params=_cparams(1),
        name="ffn",
    )(x, nw, wg, wu, wd, fw)


def _ffn_weights(w_gu, w_down):
    d, two_ff = w_gu.shape
    ff = two_ff // 2
    nc = ff // FFN_CHUNK
    wg = w_gu[:, :ff].reshape(d, nc, FFN_CHUNK).transpose(1, 0, 2).astype(BF16)
    wu = w_gu[:, ff:].reshape(d, nc, FFN_CHUNK).transpose(1, 0, 2).astype(BF16)
    wd = w_down.reshape(nc, FFN_CHUNK, d).astype(BF16)
    return wg, wu, wd


def _inproj_ab_body(x_ref, nw_ref, wq_ref, wk_ref, wv_ref, wgk_ref, wg_ref, wz_ref, wx_ref, wdt_ref,
                    gk2_ref, gkb_ref, dtb_ref,
                    q_ref, k_ref, v_ref, lf_ref, g_ref, z_ref, xbc_ref, dt_ref):
    xn = _rms(x_ref[...], nw_ref[...], 1e-6).astype(BF16)
    q_ref[...] = _dot(xn, wq_ref[...]) * (GLA_DK ** -0.5)
    k_ref[...] = _dot(xn, wk_ref[...])
    v_ref[...] = _dot(xn, wv_ref[...])
    gk_lo = _dot(xn, wgk_ref[...]).astype(BF16)
    gk = _dot(gk_lo, gk2_ref[...]) + gkb_ref[...]
    lf_ref[...] = -_softplus(-gk) / GLA_GATE_NORM
    g_ref[...] = _dot(xn, wg_ref[...])
    z_ref[...] = _dot(xn, wz_ref[...])
    xbc_ref[...] = _dot(xn, wx_ref[...])
    dt_ref[...] = _softplus(_dot(xn, wdt_ref[...]) + dtb_ref[...])


def _inproj_ab(x, nw, ws, gk2, gkb, dtb):
    n, d = x.shape
    tm = _row_tile(n, 448)
    widths = [w.shape[1] for w in ws]
    out_w = [widths[0], widths[1], widths[2], gk2.shape[1], widths[4], widths[5], widths[6], widths[7]]
    return pl.pallas_call(
        _inproj_ab_body,
        out_shape=[jax.ShapeDtypeStruct((n, w), F32) for w in out_w],
        grid=(n // tm,),
        in_specs=[pl.BlockSpec((tm, d), lambda i: (i, 0)), _resident(nw.shape)]
                 + [_resident(w.shape) for w in ws]
                 + [_resident(gk2.shape), _resident(gkb.shape), _resident(dtb.shape)],
        out_specs=[pl.BlockSpec((tm, w), lambda i: (i, 0)) for w in out_w],
        compiler_params=_cparams(1),
        name="inproj_ab",
    )(x, nw, *ws, gk2, gkb, dtb)


def _qkv_body(x_ref, nw_ref, wq_ref, wk_ref, wv_ref, q_ref, k_ref, v_ref):
    xn = _rms(x_ref[...], nw_ref[...], 1e-6).astype(BF16)
    q_ref[...] = _dot(xn, wq_ref[...]) * (DA_DH ** -0.5)
    k_ref[...] = _dot(xn, wk_ref[...])
    v_ref[...] = _dot(xn, wv_ref[...])


def _qkv(x, nw, wq, wk, wv):
    n, d = x.shape
    tm = _row_tile(n, 896)
    w = wq.shape[1]
    return pl.pallas_call(
        _qkv_body,
        out_shape=[jax.ShapeDtypeStruct((n, w), F32)] * 3,
        grid=(n // tm,),
        in_specs=[pl.BlockSpec((tm, d), lambda i: (i, 0)), _resident(nw.shape),
                  _resident(wq.shape), _resident(wk.shape), _resident(wv.shape)],
        out_specs=[pl.BlockSpec((tm, w), lambda i: (i, 0))] * 3,
        compiler_params=_cparams(1),
        name="qkv",
    )(x, nw, wq, wk, wv)


def _outproj_body(x_ref, m_ref, w_ref, o_ref):
    o_ref[...] = x_ref[...] + _dot(m_ref[...].astype(BF16), w_ref[...])


def _outproj(x, mix, w):
    n, d = x.shape
    km = mix.shape[1]
    tm = _row_tile(n, 896)
    return pl.pallas_call(
        _outproj_body,
        out_shape=jax.ShapeDtypeStruct((n, d), F32),
        grid=(n // tm,),
        in_specs=[pl.BlockSpec((tm, d), lambda i: (i, 0)), pl.BlockSpec((tm, km), lambda i: (i, 0)),
                  _resident(w.shape)],
        out_specs=pl.BlockSpec((tm, d), lambda i: (i, 0)),
        compiler_params=_cparams(1),
        name="outproj",
    )(x, mix, w)


HIST = SUBLANES


def _scan_body(q_ref, k_ref, v_ref, lf_ref, g_ref, z_ref, xbc_ref, dt_ref,
               gla0_ref, ssd0_ref, conv0_ref,
               gnorm_ref, cw_ref, cb_ref, alog_ref, dskip_ref, snorm_ref, expand_ref,
               mix_ref, gla_out_ref, ssd_out_ref, conv_out_ref,
               st_ref, ss_ref, xpad_ref, y_ref, *, c_len, n_steps):
    j = pl.program_id(1)
    kc = SSD_CONV - 1
    conv_dim = xbc_ref.shape[1]

    @pl.when(j == 0)
    def _load_state():
        for h in range(GLA_HEADS):
            st_ref[h] = gla0_ref[0, h].T
        ss_ref[...] = ssd0_ref[0]
        xpad_ref[0:HIST, :] = jnp.zeros((HIST, conv_dim), F32)
        xpad_ref[HIST - kc:HIST, :] = conv0_ref[0]

    row = lax.broadcasted_iota(jnp.int32, (c_len, c_len), 0)
    col = lax.broadcasted_iota(jnp.int32, (c_len, c_len), 1)
    tril = row >= col
    ltri = jnp.where(tril, 1.0, 0.0).astype(BF16)

    xpad_ref[HIST:HIST + c_len, :] = xbc_ref[...]
    acc = cb_ref[...] + xpad_ref[HIST - kc:HIST - kc + c_len, :] * cw_ref[0:1, :]
    for t in range(1, SSD_CONV):
        acc = acc + xpad_ref[HIST - kc + t:HIST - kc + t + c_len, :] * cw_ref[t:t + 1, :]
    tail = xpad_ref[HIST + c_len - kc:HIST + c_len, :]
    xpad_ref[HIST - kc:HIST, :] = tail
    xc = _silu(acc)
    xs = xc[:, :SSD_INNER]
    gs = SSD_GROUPS * SSD_STATE
    bmat = xc[:, SSD_INNER:SSD_INNER + gs]
    cmat = xc[:, SSD_INNER + gs:SSD_INNER + 2 * gs]

    expand = expand_ref[...]
    dt = dt_ref[...]
    da = dt * (-jnp.exp(alog_ref[...]))
    cs = _dot01_left(ltri, da)
    cs_t = cs.T
    cs_x = _dot01_right(cs, expand)
    dt_x = _dot01_right(dt, expand)
    cs_last_x = cs_x[c_len - 1:c_len, :]
    xdt = xs * dt_x
    e_cs = jnp.exp(cs_x)
    xw = xdt * jnp.exp(cs_last_x - cs_x)
    g_last = jnp.exp(cs_last_x)
    hpg = SSD_HEADS // SSD_GROUPS
    gw = hpg * SSD_HEADDIM
    for gi in range(SSD_GROUPS):
        bg = bmat[:, gi * SSD_STATE:(gi + 1) * SSD_STATE].astype(BF16)
        cg = cmat[:, gi * SSD_STATE:(gi + 1) * SSD_STATE].astype(BF16)
        gmat = _dot_nt(cg, bg)
        s_g = ss_ref[gi * gw:(gi + 1) * gw, :]
        y_inter = _dot_nt(cg, s_g.astype(BF16)) * e_cs[:, gi * gw:(gi + 1) * gw]
        ds_g = _dot_tn(xw[:, gi * gw:(gi + 1) * gw].astype(BF16), bg)
        for hh in range(hpg):
            h = gi * hpg + hh
            lo, hi = h * SSD_HEADDIM, (h + 1) * SSD_HEADDIM
            seg = jnp.broadcast_to(cs[:, h:h + 1], (c_len, c_len)) - cs_t[h:h + 1, :]
            decay = jnp.exp(jnp.where(tril, seg, NEG_INF))
            sc = (gmat * decay).astype(BF16)
            y_intra = _dot(sc, xdt[:, lo:hi].astype(BF16))
            y_ref[:, lo:hi] = y_intra + y_inter[:, hh * SSD_HEADDIM:(hh + 1) * SSD_HEADDIM]
            ss_ref[lo:hi, :] = (g_last[:, lo:hi] * s_g[hh * SSD_HEADDIM:(hh + 1) * SSD_HEADDIM, :]
                                + ds_g[hh * SSD_HEADDIM:(hh + 1) * SSD_HEADDIM, :])
    y = (y_ref[...] + xs * dskip_ref[...]) * _silu(z_ref[...])
    half = SSD_INNER // SSD_GROUPS
    for gi in range(SSD_GROUPS):
        mix_ref[:, GLA_HEADS * GLA_DV + gi * half:GLA_HEADS * GLA_DV + (gi + 1) * half] = _rms(
            y[:, gi * half:(gi + 1) * half], snorm_ref[:, gi * half:(gi + 1) * half], 1e-6)

    b = _dot01_left(ltri, lf_ref[...])
    b_last = b[c_len - 1:c_len, :]
    q_dec = q_ref[...] * jnp.exp(b)
    k = k_ref[...]
    k_inv = k * jnp.exp(-b)
    k_end = k * jnp.exp(b_last - b)
    g_row = jnp.exp(b_last)
    for h in range(GLA_HEADS):
        ks, ke = h * GLA_DK, (h + 1) * GLA_DK
        vs, ve = h * GLA_DV, (h + 1) * GLA_DV
        qd = q_dec[:, ks:ke].astype(BF16)
        att = jnp.where(tril, _dot_nt(qd, k_inv[:, ks:ke].astype(BF16)), 0.0)
        v_h = v_ref[:, vs:ve].astype(BF16)
        st = st_ref[h]
        o = _dot(att.astype(BF16), v_h) + _dot_nt(qd, st.astype(BF16))
        st_ref[h] = g_row[:, ks:ke] * st + _dot_tn(v_h, k_end[:, ks:ke].astype(BF16))
        mix_ref[:, vs:ve] = _rms(o, gnorm_ref[...], 1e-6) * _silu(g_ref[:, vs:ve])

    @pl.when(j == n_steps - 1)
    def _store_state():
        for h in range(GLA_HEADS):
            gla_out_ref[0, h] = st_ref[h].T
        ssd_out_ref[0] = ss_ref[...]
        conv_out_ref[0] = xpad_ref[HIST - kc:HIST, :]


def _scan(proj, row0, n_seq, seq_len, c_len, gla0, ssd0, conv0, params):
    q, k, v, lf, g, z, xbc, dt = proj
    n_steps = seq_len // c_len
    blk0 = row0 // c_len
    conv_dim = xbc.shape[1]
    mix_w = GLA_HEADS * GLA_DV + SSD_INNER

    def rows(w):
        return pl.BlockSpec((c_len, w), lambda b, j: (blk0 + b * n_steps + j, 0))

    def per_seq(shape):
        nd = len(shape)
        return pl.BlockSpec((1,) + tuple(shape[1:]), lambda b, j: (b,) + (0,) * (nd - 1))

    body = functools.partial(_scan_body, c_len=c_len, n_steps=n_steps)
    return pl.pallas_call(
        body,
        out_shape=[jax.ShapeDtypeStruct((n_seq * seq_len, mix_w), F32),
                   jax.ShapeDtypeStruct(gla0.shape, F32),
                   jax.ShapeDtypeStruct(ssd0.shape, F32),
                   jax.ShapeDtypeStruct(conv0.shape, F32)],
        grid=(n_seq, n_steps),
        in_specs=[rows(a.shape[1]) for a in proj]
                 + [per_seq(gla0.shape), per_seq(ssd0.shape), per_seq(conv0.shape)]
                 + [_resident(p.shape) for p in params],
        out_specs=[pl.BlockSpec((c_len, mix_w), lambda b, j: (b * n_steps + j, 0)),
                   per_seq(gla0.shape), per_seq(ssd0.shape), per_seq(conv0.shape)],
        scratch_shapes=[pltpu.VMEM((GLA_HEADS, GLA_DV, GLA_DK), F32),
                        pltpu.VMEM((SSD_INNER, SSD_STATE), F32),
                        pltpu.VMEM((HIST + c_len, conv_dim), F32),
                        pltpu.VMEM((c_len, SSD_INNER), F32)],
        compiler_params=_cparams(2),
        name=f"scan_c{c_len}",
    )(q, k, v, lf, g, z, xbc, dt, gla0, ssd0, conv0, *params)


def _softmax_step(qb, kb, vb, bias, m_ref, l_ref, acc_ref, shift=None):
    s = _dot_nt(qb, kb) + bias
    m_prev = m_ref[...]
    s_max = jnp.max(s, axis=1, keepdims=True)
    if shift is None:
        m_new = jnp.maximum(m_prev, s_max)
        off = m_new
    else:
        m_new = jnp.maximum(m_prev, s_max + shift)
        off = m_new - shift
    p = jnp.exp(s - off[:, :1])
    alpha = jnp.exp(m_prev - m_new)
    l_ref[...] = alpha * l_ref[...] + jnp.sum(p, axis=1, keepdims=True)
    acc_ref[...] = alpha * acc_ref[...] + _dot(p.astype(BF16), vb)
    m_ref[...] = m_new


def _lambda(lamv_ref, lam_init):
    lv = lamv_ref[...]
    s1 = jnp.sum(lv[0:1, :] * lv[1:2, :], axis=1, keepdims=True)
    s2 = jnp.sum(lv[2:3, :] * lv[3:4, :], axis=1, keepdims=True)
    return jnp.exp(s1) - jnp.exp(s2) + lam_init


def _pad_rows(x, n):
    return jnp.concatenate([x, jnp.zeros((n - x.shape[0], x.shape[1]), x.dtype)], axis=0)


def _split_maps(qv):
    lane = lax.broadcasted_iota(jnp.int32, qv.shape, 1)
    return jnp.concatenate([jnp.where(lane < DA_DH, qv, 0.0), jnp.where(lane >= DA_DH, qv, 0.0)], axis=0)


def _attn_prompt_body(slopes_ref, q_ref, k_ref, v_ref, km_ref, vm_ref, lamv_ref, subln_ref, o_ref,
                      qb_ref, m_ref, l_ref, acc_ref, *, tq, tk, n_meta, n_kblk, lam_init):
    h = pl.program_id(1)
    qi = pl.program_id(2)
    kj = pl.program_id(3)
    slope = slopes_ref[h]
    m_rows = 2 * tq
    q_off = qi * tq

    @pl.when(kj == 0)
    def _start():
        qb_ref[...] = _split_maps(q_ref[...]).astype(BF16)
        m_ref[...] = jnp.full(m_ref.shape, NEG_INF, F32)
        l_ref[...] = jnp.zeros(l_ref.shape, F32)
        acc_ref[...] = jnp.zeros(acc_ref.shape, F32)
        colm = lax.broadcasted_iota(jnp.int32, (1, LANES), 1)
        posm = (colm - n_meta - q_off).astype(F32)
        bias = jnp.where(colm < n_meta, slope * posm, NEG_INF)
        _softmax_step(qb_ref[...], _pad_rows(km_ref[...], LANES).astype(BF16),
                      _pad_rows(vm_ref[...], LANES).astype(BF16), bias, m_ref, l_ref, acc_ref)

    k_off = kj * tk
    last_blk = (q_off + tq - 1) // tk
    colk = lax.broadcasted_iota(jnp.int32, (1, tk), 1)
    cbias = slope * (colk + (k_off - q_off)).astype(F32)

    @pl.when(jnp.logical_and(kj <= last_blk, k_off + tk - 1 <= q_off))
    def _below_diagonal():
        _softmax_step(qb_ref[...], k_ref[...].astype(BF16), v_ref[...].astype(BF16), cbias,
                      m_ref, l_ref, acc_ref)

    @pl.when(jnp.logical_and(kj <= last_blk, k_off + tk - 1 > q_off))
    def _on_diagonal():
        r = lax.broadcasted_iota(jnp.int32, (m_rows, tk), 0)
        r = jnp.where(r >= tq, r - tq, r)
        c = lax.broadcasted_iota(jnp.int32, (m_rows, tk), 1)
        bias = jnp.where(c + k_off <= r + q_off, jnp.broadcast_to(cbias, (m_rows, tk)), NEG_INF)
        _softmax_step(qb_ref[...], k_ref[...].astype(BF16), v_ref[...].astype(BF16), bias,
                      m_ref, l_ref, acc_ref)

    @pl.when(kj == n_kblk - 1)
    def _finish():
        on = acc_ref[...] / l_ref[...]
        o = on[:tq] - _lambda(lamv_ref, lam_init) * on[tq:]
        o_ref[...] = _rms(o, subln_ref[...], 1e-5) * (1.0 - lam_init)


def _attn_prompt(q, k, v, slopes, lamv, subln, n_seq, seq_len, n_meta, meta_row0, lam_init):
    tq = _row_tile(seq_len, 256)
    tk = _row_tile(seq_len, 512)
    nq, nk = seq_len // tq, seq_len // tk
    meta_blk0 = meta_row0 // n_meta

    def kv_map(b, h, qi, kj, *_):
        return (b * nk + jnp.minimum(kj, (qi * tq + tq - 1) // tk), h)

    body = functools.partial(_attn_prompt_body, tq=tq, tk=tk, n_meta=n_meta, n_kblk=nk, lam_init=lam_init)
    grid_spec = pltpu.PrefetchScalarGridSpec(
        num_scalar_prefetch=0,
        grid=(n_seq, DA_HEADS, nq, nk),
        in_specs=[pl.BlockSpec(memory_space=pltpu.SMEM),
                  pl.BlockSpec((tq, DA_VD), lambda b, h, qi, kj: (b * nq + qi, h)),
                  pl.BlockSpec((tk, DA_VD), kv_map),
                  pl.BlockSpec((tk, DA_VD), kv_map),
                  pl.BlockSpec((n_meta, DA_VD), lambda b, h, qi, kj: (meta_blk0 + b, h)),
                  pl.BlockSpec((n_meta, DA_VD), lambda b, h, qi, kj: (meta_blk0 + b, h)),
                  pl.BlockSpec(lamv.shape, lambda b, h, qi, kj: (0, 0)),
                  pl.BlockSpec(subln.shape, lambda b, h, qi, kj: (0, 0))],
        out_specs=pl.BlockSpec((tq, DA_VD), lambda b, h, qi, kj: (b * nq + qi, h)),
        scratch_shapes=[pltpu.VMEM((2 * tq, DA_VD), BF16), pltpu.VMEM((2 * tq, LANES), F32),
                        pltpu.VMEM((2 * tq, LANES), F32), pltpu.VMEM((2 * tq, DA_VD), F32)])
    return pl.pallas_call(
        body,
        out_shape=jax.ShapeDtypeStruct((n_seq * seq_len, DA_HEADS * DA_VD), F32),
        grid_spec=grid_spec,
        compiler_params=_cparams(4),
        name="attn_prompt",
    )(slopes, q, k, v, k, v, lamv, subln)


def _attn_meta_body(slopes_ref, q_ref, k_ref, v_ref, lamv_ref, subln_ref, o_ref,
                    m_ref, l_ref, acc_ref, *, n_meta, lam_init):
    h = pl.program_id(1)
    slope = slopes_ref[h]
    m_rows = 2 * n_meta
    m_ref[...] = jnp.full(m_ref.shape, NEG_INF, F32)
    l_ref[...] = jnp.zeros(l_ref.shape, F32)
    acc_ref[...] = jnp.zeros(acc_ref.shape, F32)
    qb = _split_maps(q_ref[...]).astype(BF16)
    r = lax.broadcasted_iota(jnp.int32, (m_rows, LANES), 0)
    r = jnp.where(r >= n_meta, r - n_meta, r)
    c = lax.broadcasted_iota(jnp.int32, (m_rows, LANES), 1)
    bias = jnp.where(c <= r, slope * c.astype(F32), NEG_INF)
    _softmax_step(qb, _pad_rows(k_ref[...], LANES).astype(BF16), _pad_rows(v_ref[...], LANES).astype(BF16),
                  bias, m_ref, l_ref, acc_ref)
    on = acc_ref[...] / l_ref[...]
    o = on[:n_meta] - _lambda(lamv_ref, lam_init) * on[n_meta:]
    o_ref[...] = _rms(o, subln_ref[...], 1e-5) * (1.0 - lam_init)


def _attn_meta(q, k, v, slopes, lamv, subln, n_seq, n_meta, meta_row0, lam_init):
    blk0 = meta_row0 // n_meta
    spec = pl.BlockSpec((n_meta, DA_VD), lambda b, h: (blk0 + b, h))
    body = functools.partial(_attn_meta_body, n_meta=n_meta, lam_init=lam_init)
    return pl.pallas_call(
        body,
        out_shape=jax.ShapeDtypeStruct((n_seq * n_meta, DA_HEADS * DA_VD), F32),
        grid=(n_seq, DA_HEADS),
        in_specs=[pl.BlockSpec(memory_space=pltpu.SMEM), spec, spec, spec,
                  pl.BlockSpec(lamv.shape, lambda b, h: (0, 0)),
                  pl.BlockSpec(subln.shape, lambda b, h: (0, 0))],
        out_specs=pl.BlockSpec((n_meta, DA_VD), lambda b, h: (b, h)),
        scratch_shapes=[pltpu.VMEM((2 * n_meta, LANES), F32), pltpu.VMEM((2 * n_meta, LANES), F32),
                        pltpu.VMEM((2 * n_meta, DA_VD), F32)],
        compiler_params=_cparams(2),
        name="attn_meta",
    )(slopes, q, k, v, lamv, subln)


def _attn_sample_body(pt_ref, q_ref, kn_ref, vn_ref, ck_ref, cv_ref, base_ref, basen_ref, rslope_ref,
                      lamv_ref, subln_ref, o_ref, qb_ref, m_ref, l_ref, acc_ref, *,
                      dec_len, n_pages, lam_init):
    p = pl.program_id(1)
    rows_per_head = 2 * dec_len

    @pl.when(p == 0)
    def _start():
        lane = lax.broadcasted_iota(jnp.int32, (dec_len, DA_VD), 1)
        for h in range(DA_HEADS):
            qh = q_ref[:, h * DA_VD:(h + 1) * DA_VD]
            qb_ref[h * rows_per_head:h * rows_per_head + dec_len, :] = jnp.where(lane < DA_DH, qh, 0.0).astype(BF16)
            qb_ref[h * rows_per_head + dec_len:(h + 1) * rows_per_head, :] = jnp.where(lane >= DA_DH, qh, 0.0).astype(BF16)
        m_ref[...] = jnp.full(m_ref.shape, NEG_INF, F32)
        l_ref[...] = jnp.zeros(l_ref.shape, F32)
        acc_ref[...] = jnp.zeros(acc_ref.shape, F32)

    shift = rslope_ref[...] * ((p - n_pages) * PAGE).astype(F32)
    _softmax_step(qb_ref[...], ck_ref[0].astype(BF16), cv_ref[0].astype(BF16), base_ref[...],
                  m_ref, l_ref, acc_ref, shift=shift)

    @pl.when(p == n_pages - 1)
    def _finish():
        _softmax_step(qb_ref[...], _pad_rows(kn_ref[0], LANES).astype(BF16),
                      _pad_rows(vn_ref[0], LANES).astype(BF16), basen_ref[...], m_ref, l_ref, acc_ref)
        on = acc_ref[...] / l_ref[...]
        lam = _lambda(lamv_ref, lam_init)
        for h in range(DA_HEADS):
            o = (on[h * rows_per_head:h * rows_per_head + dec_len]
                 - lam * on[h * rows_per_head + dec_len:(h + 1) * rows_per_head])
            o_ref[:, h * DA_VD:(h + 1) * DA_VD] = _rms(o, subln_ref[...], 1e-5) * (1.0 - lam_init)


def _attn_sample(q, kn, vn, ck, cv, page_table, lamv, subln, n_seq, dec_len, samp_row0, lam_init):
    n_pages = page_table.shape[1]
    rows = DA_HEADS * 2 * dec_len
    assert rows == LANES and DA_HEADS * dec_len <= LANES
    page_rows = PAGE * DA_HEADS
    slopes = 2.0 ** (-8.0 * (jnp.arange(rows) // (2 * dec_len) + 1) / DA_HEADS)
    rhead = jnp.arange(rows) // (2 * dec_len)
    rq = jnp.arange(rows) % dec_len
    cpos = jnp.arange(page_rows) // DA_HEADS
    chead = jnp.arange(page_rows) % DA_HEADS
    base = jnp.where(rhead[:, None] == chead[None, :], slopes[:, None] * cpos[None, :].astype(F32), NEG_INF)
    cn = jnp.arange(LANES)
    valid = (cn[None, :] < DA_HEADS * dec_len) & (rhead[:, None] == (cn % DA_HEADS)[None, :]) \
        & ((cn // DA_HEADS)[None, :] <= rq[:, None])
    basen = jnp.where(valid, slopes[:, None] * (cn // DA_HEADS)[None, :].astype(F32), NEG_INF)
    rslope = jnp.broadcast_to(slopes[:, None], (rows, LANES)).astype(F32)
    blk0 = samp_row0 // dec_len
    body = functools.partial(_attn_sample_body, dec_len=dec_len, n_pages=n_pages, lam_init=lam_init)
    width = DA_HEADS * DA_VD
    grid_spec = pltpu.PrefetchScalarGridSpec(
        num_scalar_prefetch=1,
        grid=(n_seq, n_pages),
        in_specs=[pl.BlockSpec((dec_len, width), lambda i, p, pt: (blk0 + i, 0)),
                  pl.BlockSpec((1,) + kn.shape[1:], lambda i, p, pt: (i, 0, 0)),
                  pl.BlockSpec((1,) + vn.shape[1:], lambda i, p, pt: (i, 0, 0)),
                  pl.BlockSpec((1, page_rows, DA_VD), lambda i, p, pt: (pt[i, p], 0, 0)),
                  pl.BlockSpec((1, page_rows, DA_VD), lambda i, p, pt: (pt[i, p], 0, 0)),
                  pl.BlockSpec(base.shape, lambda i, p, pt: (0, 0)),
                  pl.BlockSpec(basen.shape, lambda i, p, pt: (0, 0)),
                  pl.BlockSpec(rslope.shape, lambda i, p, pt: (0, 0)),
                  pl.BlockSpec(lamv.shape, lambda i, p, pt: (0, 0)),
                  pl.BlockSpec(subln.shape, lambda i, p, pt: (0, 0))],
        out_specs=pl.BlockSpec((dec_len, width), lambda i, p, pt: (i, 0)),
        scratch_shapes=[pltpu.VMEM((rows, DA_VD), BF16), pltpu.VMEM((rows, LANES), F32),
                        pltpu.VMEM((rows, LANES), F32), pltpu.VMEM((rows, DA_VD), F32)])
    return pl.pallas_call(
        body,
        out_shape=jax.ShapeDtypeStruct((n_seq * dec_len, width), F32),
        grid_spec=grid_spec,
        compiler_params=_cparams(2),
        name="attn_sample",
    )(page_table, q, kn, vn, ck, cv, base.astype(F32), basen.astype(F32), rslope, lamv, subln)


def kernel(x_prompt, x_sample, state_gla, state_ssd, state_conv, cache_k, cache_v, page_table, meta_tokens, norm_ffn1, w_ffn1_gu, w_ffn1_down, norm_mix, norm_ffn2, w_ffn2_gu, w_ffn2_down, w_in_ab, gla_gk_w2, gla_gk_b, gla_norm, ssd_conv_w, ssd_conv_b, ssd_dt_bias, ssd_a_log, ssd_d, ssd_norm, w_out_ab, w_qkv_c, lam_q1, lam_k1, lam_q2, lam_k2, subln_c, w_out_c, norm_final):
    n_b, seq, d = x_prompt.shape
    n_s, dec, _ = x_sample.shape
    n_meta = meta_tokens.shape[0]
    depth = norm_ffn1.shape[0]
    assert seq % CHUNK == 0 and n_meta % SUBLANES == 0 and dec % SUBLANES == 0
    main_rows, meta_rows, samp_rows = n_b * seq, n_b * n_meta, n_s * dec
    meta_row0, samp_row0 = main_rows, main_rows + meta_rows

    x = jnp.concatenate([x_prompt.reshape(main_rows, d),
                         jnp.tile(meta_tokens.astype(F32), (n_b, 1)),
                         x_sample.reshape(samp_rows, d)], axis=0)

    row = lambda a: a.reshape(1, -1).astype(F32)
    gla_p, gla_s, ssd_p, ssd_s, conv_p, conv_s = [], [], [], [], [], []
    k_p, v_p, k_s, v_s = [], [], [], []
    conv_dim = ssd_conv_w.shape[2]
    gqk, gv = GLA_HEADS * GLA_DK, GLA_HEADS * GLA_DV
    bounds = [0]
    for w in (gqk, gqk, gv, GLA_RANK, gv, SSD_INNER, conv_dim, SSD_HEADS):
        bounds.append(bounds[-1] + w)
    expand = (jnp.arange(SSD_INNER)[None, :] // SSD_HEADDIM == jnp.arange(SSD_HEADS)[:, None]).astype(BF16)

    for l in range(depth):
        x = _ffn(x, row(norm_ffn1[l]), *_ffn_weights(w_ffn1_gu[l], w_ffn1_down[l]), row(norm_final), False)
        if l % 2 == 0:
            e = l // 2
            ws = [w_in_ab[e][:, bounds[i]:bounds[i + 1]].astype(BF16) for i in range(8)]
            proj = _inproj_ab(x, row(norm_mix[l]), ws, gla_gk_w2[e].astype(BF16), row(gla_gk_b[e]),
                              row(ssd_dt_bias[e]))
            params = (row(gla_norm[e]), ssd_conv_w[e].astype(F32), row(ssd_conv_b[e]), row(ssd_a_log[e]),
                      row(jnp.repeat(ssd_d[e], SSD_HEADDIM)), row(ssd_norm[e]), expand)
            ssd_shape = (SSD_INNER, SSD_STATE)
            zg = jnp.zeros((n_b, GLA_HEADS, GLA_DK, GLA_DV), F32)
            zs = jnp.zeros((n_b,) + ssd_shape, F32)
            zc = jnp.zeros((n_b, SSD_CONV - 1, conv_dim), F32)
            mix_meta, sg, ss, cv = _scan(proj, meta_row0, n_b, n_meta, n_meta, zg, zs, zc, params)
            mix_main, sg, ss, cv = _scan(proj, 0, n_b, seq, CHUNK, sg, ss, cv, params)
            gla_p.append(sg)
            ssd_p.append(ss.reshape(n_b, SSD_HEADS, SSD_HEADDIM, SSD_STATE))
            conv_p.append(cv)
            mix_samp, sg, ss, cv = _scan(proj, samp_row0, n_s, dec, dec, state_gla[e].astype(F32),
                                         state_ssd[e].astype(F32).reshape((n_s,) + ssd_shape),
                                         state_conv[e].astype(F32), params)
            gla_s.append(sg)
            ssd_s.append(ss.reshape(n_s, SSD_HEADS, SSD_HEADDIM, SSD_STATE))
            conv_s.append(cv)
            mix = jnp.concatenate([mix_main, mix_meta, mix_samp], axis=0)
            x = _outproj(x, mix, w_out_ab[e].astype(BF16))
        else:
            o = l // 2
            lam_init = 0.8 - 0.6 * math.exp(-0.3 * l)
            width = DA_HEADS * DA_VD
            wq, wk, wv = (w_qkv_c[o][:, i * width:(i + 1) * width].astype(BF16) for i in range(3))
            q, k, v = _qkv(x, row(norm_mix[l]), wq, wk, wv)
            slopes = jnp.exp2(-8.0 * jnp.arange(1, DA_HEADS + 1, dtype=F32) / DA_HEADS)
            lamv = jnp.stack([lam_q1[o], lam_k1[o], lam_q2[o], lam_k2[o]]).astype(F32)
            subln = row(subln_c[o])
            o_main = _attn_prompt(q, k, v, slopes, lamv, subln, n_b, seq, n_meta, meta_row0, lam_init)
            o_meta = _attn_meta(q, k, v, slopes, lamv, subln, n_b, n_meta, meta_row0, lam_init)
            kn = k[samp_row0:].reshape(n_s, dec * DA_HEADS, DA_VD)
            vn = v[samp_row0:].reshape(n_s, dec * DA_HEADS, DA_VD)
            ck = cache_k[o].reshape(cache_k.shape[1], PAGE * DA_HEADS, DA_VD)
            cv_ = cache_v[o].reshape(cache_v.shape[1], PAGE * DA_HEADS, DA_VD)
            o_samp = _attn_sample(q, kn, vn, ck, cv_, page_table, lamv, subln, n_s, dec, samp_row0, lam_init)
            x = _outproj(x, jnp.concatenate([o_main, o_meta, o_samp], axis=0), w_out_c[o].astype(BF16))
            rows5 = lambda a, n, t: a.reshape(n, t, DA_HEADS, DA_VD)
            k_p.append(jnp.concatenate([rows5(k[meta_row0:samp_row0], n_b, n_meta), rows5(k[:main_rows], n_b, seq)], axis=1))
            v_p.append(jnp.concatenate([rows5(v[meta_row0:samp_row0], n_b, n_meta), rows5(v[:main_rows], n_b, seq)], axis=1))
            k_s.append(rows5(k[samp_row0:], n_s, dec))
            v_s.append(rows5(v[samp_row0:], n_s, dec))
        x = _ffn(x, row(norm_ffn2[l]), *_ffn_weights(w_ffn2_gu[l], w_ffn2_down[l]), row(norm_final),
                 l == depth - 1)

    y_prompt = x[:main_rows].reshape(n_b, seq, d)
    y_sample = x[samp_row0:].reshape(n_s, dec, d)
    return (y_prompt, y_sample, jnp.stack(gla_p), jnp.stack(gla_s), jnp.stack(ssd_p), jnp.stack(ssd_s),
            jnp.stack(conv_p), jnp.stack(conv_s), jnp.stack(k_p), jnp.stack(v_p), jnp.stack(k_s), jnp.stack(v_s))
```

```python
import functools
import math

import jax
import jax.numpy as jnp
from jax import lax
from jax.experimental import pallas as pl
from jax.experimental.pallas import tpu as pltpu

F32 = jnp.float32
BF16 = jnp.bfloat16

CHUNK = 64
GLA_HEADS, GLA_DK, GLA_DV, GLA_RANK = 4, 128, 256, 16
GLA_GATE_NORM = 16.0
SSD_HEADS, SSD_HEADDIM, SSD_STATE, SSD_GROUPS, SSD_CONV = 16, 64, 64, 2, 4
SSD_INNER = SSD_HEADS * SSD_HEADDIM
DA_HEADS, DA_DH = 8, 64
DA_VD = 2 * DA_DH
PAGE = 128

LANES = 128
SUBLANES = 8
VMEM_LIMIT = 56 * 1024 * 1024

NEG_INF = float("-inf")


def _cparams(n_axes):
    return pltpu.CompilerParams(dimension_semantics=("arbitrary",) * n_axes,
                                vmem_limit_bytes=VMEM_LIMIT)


def _row_tile(n, target):
    best = SUBLANES
    for t in range(SUBLANES, target + 1, SUBLANES):
        if n % t == 0:
            best = t
    return best


def _resident(shape):
    nd = len(shape)
    return pl.BlockSpec(shape, lambda *_: (0,) * nd, pipeline_mode=pl.Buffered(1))


def _rms(x, w, eps):
    ms = jnp.mean(x * x, axis=-1, keepdims=True)
    return x * lax.rsqrt(ms + eps) * w


def _silu(x):
    return x * jax.nn.sigmoid(x)


def _softplus(x):
    return jnp.maximum(x, 0.0) + jnp.log1p(jnp.exp(-jnp.abs(x)))


def _dot(a, b):
    return jnp.dot(a, b, preferred_element_type=F32)


def _dot_nt(a, b):
    return lax.dot_general(a, b, (((1,), (1,)), ((), ())), preferred_element_type=F32)


def _dot_tn(a, b):
    return lax.dot_general(a, b, (((0,), (0,)), ((), ())), preferred_element_type=F32)


def _split3(a):
    hi = a.astype(BF16)
    r1 = a - hi.astype(F32)
    mid = r1.astype(BF16)
    lo = (r1 - mid.astype(F32)).astype(BF16)
    return hi, mid, lo


def _dot01_left(m01, a):
    hi, mid, lo = _split3(a)
    return _dot(m01, hi) + _dot(m01, mid) + _dot(m01, lo)


def _dot01_right(a, m01):
    hi, mid, lo = _split3(a)
    return _dot(hi, m01) + _dot(mid, m01) + _dot(lo, m01)


FFN_CHUNK = 256


def _ffn_body(x_ref, nw_ref, wg_ref, wu_ref, wd_ref, fw_ref, o_ref, acc_ref, xn_ref, *,
              n_chunks, final_norm):
    x = x_ref[...]
    xn_ref[...] = _rms(x, nw_ref[...], 1e-6).astype(BF16)
    acc_ref[...] = jnp.zeros_like(acc_ref)

    def chunk(c, carry):
        xn = xn_ref[...]
        g = _dot(xn, wg_ref[c])
        u = _dot(xn, wu_ref[c])
        a = (_silu(g) * u).astype(BF16)
        acc_ref[...] += _dot(a, wd_ref[c])
        return carry

    lax.fori_loop(0, n_chunks, chunk, 0)
    y = x_ref[...] + 0.5 * acc_ref[...]
    if final_norm:
        y = _rms(y, fw_ref[...], 1e-6)
    o_ref[...] = y


def _ffn(x, nw, wg, wu, wd, fw, final_norm):
    n, d = x.shape
    n_chunks = wg.shape[0]
    tm = _row_tile(n, 896)
    body = functools.partial(_ffn_body, n_chunks=n_chunks, final_norm=final_norm)
    return pl.pallas_call(
        body,
        out_shape=jax.ShapeDtypeStruct((n, d), F32),
        grid=(n // tm,),
        in_specs=[pl.BlockSpec((tm, d), lambda i: (i, 0)),
                  _resident(nw.shape), _resident(wg.shape), _resident(wu.shape),
                  _resident(wd.shape), _resident(fw.shape)],
        out_specs=pl.BlockSpec((tm, d), lambda i: (i, 0)),
        scratch_shapes=[pltpu.VMEM((tm, d), F32), pltpu.VMEM((tm, d), BF16)],
        compiler_params=_cparams(1),
        name="ffn",
    )(x, nw, wg, wu, wd, fw)


def _ffn_weights(w_gu, w_down):
    d, two_ff = w_gu.shape
    ff = two_ff // 2
    nc = ff // FFN_CHUNK
    wg = w_gu[:, :ff].reshape(d, nc, FFN_CHUNK).transpose(1, 0, 2).astype(BF16)
    wu = w_gu[:, ff:].reshape(d, nc, FFN_CHUNK).transpose(1, 0, 2).astype(BF16)
    wd = w_down.reshape(nc, FFN_CHUNK, d).astype(BF16)
    return wg, wu, wd


def _inproj_ab_body(x_ref, nw_ref, wq_ref, wk_ref, wv_ref, wgk_ref, wg_ref, wz_ref, wx_ref, wdt_ref,
                    gk2_ref, gkb_ref, dtb_ref,
                    q_ref, k_ref, v_ref, lf_ref, g_ref, z_ref, xbc_ref, dt_ref):
    xn = _rms(x_ref[...], nw_ref[...], 1e-6).astype(BF16)
    q_ref[...] = _dot(xn, wq_ref[...]) * (GLA_DK ** -0.5)
    k_ref[...] = _dot(xn, wk_ref[...])
    v_ref[...] = _dot(xn, wv_ref[...])
    gk_lo = _dot(xn, wgk_ref[...]).astype(BF16)
    gk = _dot(gk_lo, gk2_ref[...]) + gkb_ref[...]
    lf_ref[...] = -_softplus(-gk) / GLA_GATE_NORM
    g_ref[...] = _dot(xn, wg_ref[...])
    z_ref[...] = _dot(xn, wz_ref[...])
    xbc_ref[...] = _dot(xn, wx_ref[...])
    dt_ref[...] = _softplus(_dot(xn, wdt_ref[...]) + dtb_ref[...])


def _inproj_ab(x, nw, ws, gk2, gkb, dtb):
    n, d = x.shape
    tm = _row_tile(n, 448)
    widths = [w.shape[1] for w in ws]
    out_w = [widths[0], widths[1], widths[2], gk2.shape[1], widths[4], widths[5], widths[6], widths[7]]
    return pl.pallas_call(
        _inproj_ab_body,
        out_shape=[jax.ShapeDtypeStruct((n, w), F32) for w in out_w],
        grid=(n // tm,),
        in_specs=[pl.BlockSpec((tm, d), lambda i: (i, 0)), _resident(nw.shape)]
                 + [_resident(w.shape) for w in ws]
                 + [_resident(gk2.shape), _resident(gkb.shape), _resident(dtb.shape)],
        out_specs=[pl.BlockSpec((tm, w), lambda i: (i, 0)) for w in out_w],
        compiler_params=_cparams(1),
        name="inproj_ab",
    )(x, nw, *ws, gk2, gkb, dtb)


LOG2E = 1.4426950408889634


def _qkv_body(x_ref, nw_ref, wq_ref, wk_ref, wv_ref, q_ref, k_ref, v_ref, kb_ref):
    xn = _rms(x_ref[...], nw_ref[...], 1e-6).astype(BF16)
    q_ref[...] = (_dot(xn, wq_ref[...]) * (DA_DH ** -0.5 * LOG2E)).astype(BF16)
    k = _dot(xn, wk_ref[...])
    k_ref[...] = k
    kb_ref[...] = k.astype(BF16)
    v_ref[...] = _dot(xn, wv_ref[...])


def _qkv(x, nw, wq, wk, wv):
    n, d = x.shape
    tm = _row_tile(n, 896)
    w = wq.shape[1]
    return pl.pallas_call(
        _qkv_body,
        out_shape=[jax.ShapeDtypeStruct((n, w), BF16), jax.ShapeDtypeStruct((n, w), F32),
                   jax.ShapeDtypeStruct((n, w), F32), jax.ShapeDtypeStruct((n, w), BF16)],
        grid=(n // tm,),
        in_specs=[pl.BlockSpec((tm, d), lambda i: (i, 0)), _resident(nw.shape),
                  _resident(wq.shape), _resident(wk.shape), _resident(wv.shape)],
        out_specs=[pl.BlockSpec((tm, w), lambda i: (i, 0))] * 4,
        compiler_params=_cparams(1),
        name="qkv",
    )(x, nw, wq, wk, wv)


def _outproj_body(x_ref, m_ref, w_ref, o_ref):
    o_ref[...] = x_ref[...] + _dot(m_ref[...].astype(BF16), w_ref[...])


def _outproj(x, mix, w, row0):
    n, d = x.shape
    ns, km = mix.shape
    tm = _row_tile(math.gcd(ns, row0) if row0 else ns, 1024)
    blk0 = row0 // tm
    return pl.pallas_call(
        _outproj_body,
        out_shape=jax.ShapeDtypeStruct((n, d), F32),
        grid=(ns // tm,),
        in_specs=[pl.BlockSpec((tm, d), lambda i: (blk0 + i, 0)), pl.BlockSpec((tm, km), lambda i: (i, 0)),
                  _resident(w.shape)],
        out_specs=pl.BlockSpec((tm, d), lambda i: (blk0 + i, 0)),
        input_output_aliases={0: 0},
        compiler_params=_cparams(1),
        name="outproj",
    )(x, mix, w)


HIST = SUBLANES


def _scan_body(q_ref, k_ref, v_ref, lf_ref, g_ref, z_ref, xbc_ref, dt_ref,
               gla0_ref, ssd0_ref, conv0_ref,
               gnorm_ref, cw_ref, cb_ref, alog_ref, dskip_ref, snorm_ref, expand_ref,
               mix_ref, gla_out_ref, ssd_out_ref, conv_out_ref,
               st_ref, ss_ref, xpad_ref, y_ref, *, c_len, n_steps):
    j = pl.program_id(1)
    kc = SSD_CONV - 1
    conv_dim = xbc_ref.shape[1]

    @pl.when(j == 0)
    def _load_state():
        for h in range(GLA_HEADS):
            st_ref[h] = gla0_ref[0, h].T
        ss_ref[...] = ssd0_ref[0]
        xpad_ref[0:HIST, :] = jnp.zeros((HIST, conv_dim), F32)
        xpad_ref[HIST - kc:HIST, :] = conv0_ref[0]

    row = lax.broadcasted_iota(jnp.int32, (c_len, c_len), 0)
    col = lax.broadcasted_iota(jnp.int32, (c_len, c_len), 1)
    tril = row >= col
    ltri = jnp.where(tril, 1.0, 0.0).astype(BF16)

    xpad_ref[HIST:HIST + c_len, :] = xbc_ref[...]
    acc = cb_ref[...] + xpad_ref[HIST - kc:HIST - kc + c_len, :] * cw_ref[0:1, :]
    for t in range(1, SSD_CONV):
        acc = acc + xpad_ref[HIST - kc + t:HIST - kc + t + c_len, :] * cw_ref[t:t + 1, :]
    tail = xpad_ref[HIST + c_len - kc:HIST + c_len, :]
    xpad_ref[HIST - kc:HIST, :] = tail
    xc = _silu(acc)
    xs = xc[:, :SSD_INNER]
    gs = SSD_GROUPS * SSD_STATE
    bmat = xc[:, SSD_INNER:SSD_INNER + gs]
    cmat = xc[:, SSD_INNER + gs:SSD_INNER + 2 * gs]

    expand = expand_ref[...]
    dt = dt_ref[...]
    da = dt * (-jnp.exp(alog_ref[...]))
    cs = _dot01_left(ltri, da)
    cs_t = cs.T
    cs_x = _dot01_right(cs, expand)
    dt_x = _dot01_right(dt, expand)
    cs_last_x = cs_x[c_len - 1:c_len, :]
    xdt = xs * dt_x
    e_cs = jnp.exp(cs_x)
    xw = xdt * jnp.exp(cs_last_x - cs_x)
    g_last = jnp.exp(cs_last_x)
    hpg = SSD_HEADS // SSD_GROUPS
    gw = hpg * SSD_HEADDIM
    for gi in range(SSD_GROUPS):
        bg = bmat[:, gi * SSD_STATE:(gi + 1) * SSD_STATE].astype(BF16)
        cg = cmat[:, gi * SSD_STATE:(gi + 1) * SSD_STATE].astype(BF16)
        gmat = _dot_nt(cg, bg)
        s_g = ss_ref[gi * gw:(gi + 1) * gw, :]
        y_inter = _dot_nt(cg, s_g.astype(BF16)) * e_cs[:, gi * gw:(gi + 1) * gw]
        ds_g = _dot_tn(xw[:, gi * gw:(gi + 1) * gw].astype(BF16), bg)
        for hh in range(hpg):
            h = gi * hpg + hh
            lo, hi = h * SSD_HEADDIM, (h + 1) * SSD_HEADDIM
            seg = jnp.broadcast_to(cs[:, h:h + 1], (c_len, c_len)) - cs_t[h:h + 1, :]
            decay = jnp.exp(jnp.where(tril, seg, NEG_INF))
            sc = (gmat * decay).astype(BF16)
            y_intra = _dot(sc, xdt[:, lo:hi].astype(BF16))
            y_ref[:, lo:hi] = y_intra + y_inter[:, hh * SSD_HEADDIM:(hh + 1) * SSD_HEADDIM]
            ss_ref[lo:hi, :] = (g_last[:, lo:hi] * s_g[hh * SSD_HEADDIM:(hh + 1) * SSD_HEADDIM, :]
                                + ds_g[hh * SSD_HEADDIM:(hh + 1) * SSD_HEADDIM, :])
    y = (y_ref[...] + xs * dskip_ref[...]) * _silu(z_ref[...])
    half = SSD_INNER // SSD_GROUPS
    for gi in range(SSD_GROUPS):
        mix_ref[:, GLA_HEADS * GLA_DV + gi * half:GLA_HEADS * GLA_DV + (gi + 1) * half] = _rms(
            y[:, gi * half:(gi + 1) * half], snorm_ref[:, gi * half:(gi + 1) * half], 1e-6)

    b = _dot01_left(ltri, lf_ref[...])
    b_last = b[c_len - 1:c_len, :]
    q_dec = q_ref[...] * jnp.exp(b)
    k = k_ref[...]
    k_inv = k * jnp.exp(-b)
    k_end = k * jnp.exp(b_last - b)
    g_row = jnp.exp(b_last)
    for h in range(GLA_HEADS):
        ks, ke = h * GLA_DK, (h + 1) * GLA_DK
        vs, ve = h * GLA_DV, (h + 1) * GLA_DV
        qd = q_dec[:, ks:ke].astype(BF16)
        att = jnp.where(tril, _dot_nt(qd, k_inv[:, ks:ke].astype(BF16)), 0.0)
        v_h = v_ref[:, vs:ve].astype(BF16)
        st = st_ref[h]
        o = _dot(att.astype(BF16), v_h) + _dot_nt(qd, st.astype(BF16))
        st_ref[h] = g_row[:, ks:ke] * st + _dot_tn(v_h, k_end[:, ks:ke].astype(BF16))
        mix_ref[:, vs:ve] = _rms(o, gnorm_ref[...], 1e-6) * _silu(g_ref[:, vs:ve])

    @pl.when(j == n_steps - 1)
    def _store_state():
        for h in range(GLA_HEADS):
            gla_out_ref[0, h] = st_ref[h].T
        ssd_out_ref[0] = ss_ref[...]
        conv_out_ref[0] = xpad_ref[HIST - kc:HIST, :]


def _scan(proj, row0, n_seq, seq_len, c_len, gla0, ssd0, conv0, params):
    q, k, v, lf, g, z, xbc, dt = proj
    n_steps = seq_len // c_len
    blk0 = row0 // c_len
    conv_dim = xbc.shape[1]
    mix_w = GLA_HEADS * GLA_DV + SSD_INNER

    def rows(w):
        return pl.BlockSpec((c_len, w), lambda b, j: (blk0 + b * n_steps + j, 0))

    def per_seq(shape):
        nd = len(shape)
        return pl.BlockSpec((1,) + tuple(shape[1:]), lambda b, j: (b,) + (0,) * (nd - 1))

    body = functools.partial(_scan_body, c_len=c_len, n_steps=n_steps)
    return pl.pallas_call(
        body,
        out_shape=[jax.ShapeDtypeStruct((n_seq * seq_len, mix_w), F32),
                   jax.ShapeDtypeStruct(gla0.shape, F32),
                   jax.ShapeDtypeStruct(ssd0.shape, F32),
                   jax.ShapeDtypeStruct(conv0.shape, F32)],
        grid=(n_seq, n_steps),
        in_specs=[rows(a.shape[1]) for a in proj]
                 + [per_seq(gla0.shape), per_seq(ssd0.shape), per_seq(conv0.shape)]
                 + [_resident(p.shape) for p in params],
        out_specs=[pl.BlockSpec((c_len, mix_w), lambda b, j: (b * n_steps + j, 0)),
                   per_seq(gla0.shape), per_seq(ssd0.shape), per_seq(conv0.shape)],
        scratch_shapes=[pltpu.VMEM((GLA_HEADS, GLA_DV, GLA_DK), F32),
                        pltpu.VMEM((SSD_INNER, SSD_STATE), F32),
                        pltpu.VMEM((HIST + c_len, conv_dim), F32),
                        pltpu.VMEM((c_len, SSD_INNER), F32)],
        compiler_params=_cparams(2),
        name=f"scan_c{c_len}",
    )(q, k, v, lf, g, z, xbc, dt, gla0, ssd0, conv0, *params)


def _softmax_step(blocks, m_ref, l_ref, acc_ref, shift=None):
    m_prev = m_ref[...]
    s_max = None
    for s, _ in blocks:
        bm = jnp.max(s, axis=1, keepdims=True)
        s_max = bm if s_max is None else jnp.maximum(s_max, bm)
    if shift is None:
        m_new = jnp.maximum(m_prev, s_max)
        off = m_new
    else:
        m_new = jnp.maximum(m_prev, s_max + shift)
        off = m_new - shift
    alpha = jnp.exp2(m_prev - m_new)
    l = alpha * l_ref[...]
    acc = alpha * acc_ref[...]
    for s, v in blocks:
        p = jnp.exp2(s - off[:, :1])
        l = l + jnp.sum(p, axis=1, keepdims=True)
        acc = acc + _dot(p.astype(BF16), v)
    l_ref[...] = l
    acc_ref[...] = acc
    m_ref[...] = m_new


def _softmax_step_t(blocks, m_ref, acc_ref, hh):
    m_prev = m_ref[hh]
    m_new = m_prev
    for s, _ in blocks:
        m_new = jnp.maximum(m_new, jnp.max(s, axis=0, keepdims=True))
    acc = jnp.exp2(m_prev - m_new) * acc_ref[hh]
    for s, vt1 in blocks:
        acc = acc + _dot(vt1, jnp.exp2(s - m_new).astype(BF16))
    acc_ref[hh] = acc
    m_ref[hh] = m_new


def _lambda(lamv_ref, lam_init):
    lv = lamv_ref[...]
    s1 = jnp.sum(lv[0:1, :] * lv[1:2, :], axis=1, keepdims=True)
    s2 = jnp.sum(lv[2:3, :] * lv[3:4, :], axis=1, keepdims=True)
    return jnp.exp(s1) - jnp.exp(s2) + lam_init


def _pad_rows(x, n):
    return jnp.concatenate([x, jnp.zeros((n - x.shape[0], x.shape[1]), x.dtype)], axis=0)


def _split_maps(qv):
    lane = lax.broadcasted_iota(jnp.int32, qv.shape, 1)
    return jnp.concatenate([jnp.where(lane < DA_DH, qv, 0.0), jnp.where(lane >= DA_DH, qv, 0.0)], axis=0)


ATT_HEADS_PER_STEP = 2
ONES_ROWS = 16
POS_RADIX = 64


def _attn_prompt_body(q_ref, k_ref, v_ref, km_ref, vm_ref, posk_ref, poskm_ref, qbias_ref, dmask_ref, mmask_ref,
                      lamv_ref, subln_ref, o_ref, vt_ref, vtm_ref, q2_ref, m_ref, acc_ref, *, tq, lam_init):
    qi = pl.program_id(2)
    seq = k_ref.shape[0]
    dv1 = DA_VD + ONES_ROWS

    @pl.when(qi == 0)
    def _transpose_v():
        for hh in range(ATT_HEADS_PER_STEP):
            lanes = slice(hh * DA_VD, (hh + 1) * DA_VD)
            for c in range(seq // tq):
                vt_ref[hh, 0:DA_VD, c * tq:(c + 1) * tq] = v_ref[c * tq:(c + 1) * tq, lanes].T.astype(BF16)
            vt_ref[hh, DA_VD:dv1, :] = jnp.ones((ONES_ROWS, seq), BF16)
            vtm_ref[hh, 0:DA_VD, :] = _pad_rows(vm_ref[:, lanes], LANES).T.astype(BF16)
            vtm_ref[hh, DA_VD:dv1, :] = jnp.ones((ONES_ROWS, LANES), BF16)

    for hh in range(ATT_HEADS_PER_STEP):
        lanes = slice(hh * DA_VD, (hh + 1) * DA_VD)
        q2_ref[hh, :, 0:DA_VD] = _split_maps(q_ref[:, lanes])
        q2_ref[hh, :, DA_VD:2 * DA_VD] = jnp.broadcast_to(qbias_ref[hh], (2 * tq, LANES))
        m_ref[hh] = jnp.full((1, 2 * tq), NEG_INF, F32)
        acc_ref[hh] = jnp.zeros((dv1, 2 * tq), F32)

    def keys(r0, hh):
        return jnp.concatenate([k_ref[pl.ds(r0, tq), hh * DA_VD:(hh + 1) * DA_VD], posk_ref[pl.ds(r0, tq), :]],
                               axis=1)

    def below_diagonal(j, carry):
        r0 = pl.multiple_of(j * tq, tq)
        for hh in range(ATT_HEADS_PER_STEP):
            s = _dot_nt(keys(r0, hh), q2_ref[hh])
            _softmax_step_t([(s, vt_ref[hh, :, pl.ds(r0, tq)])], m_ref, acc_ref, hh)
        return carry

    lax.fori_loop(0, qi, below_diagonal, 0)

    r0 = pl.multiple_of(qi * tq, tq)
    lam = _lambda(lamv_ref, lam_init)
    for hh in range(ATT_HEADS_PER_STEP):
        lanes = slice(hh * DA_VD, (hh + 1) * DA_VD)
        s_d = _dot_nt(keys(r0, hh), q2_ref[hh]) + dmask_ref[...]
        kpm = jnp.concatenate([_pad_rows(km_ref[:, lanes], LANES), poskm_ref[...]], axis=1)
        s_m = _dot_nt(kpm, q2_ref[hh]) + mmask_ref[...]
        _softmax_step_t([(s_d, vt_ref[hh, :, pl.ds(r0, tq)]), (s_m, vtm_ref[hh])], m_ref, acc_ref, hh)
        a = acc_ref[hh]
        on = a[0:DA_VD] / a[DA_VD:DA_VD + 1]
        o_t = on[:, :tq] - lam * on[:, tq:]
        o_ref[:, lanes] = _rms(o_t.T, subln_ref[...], 1e-5) * (1.0 - lam_init)


def _alibi_tables(seq_len, n_meta, tq):
    assert seq_len + n_meta <= POS_RADIX * 256
    slopes = jnp.exp2(-8.0 * jnp.arange(1, DA_HEADS + 1, dtype=F32) / DA_HEADS) * LOG2E
    hi, mid, lo = (t.astype(F32) for t in _split3(slopes))
    qbias = jnp.zeros((DA_HEADS, 1, LANES), F32)
    for i, t in enumerate((hi * POS_RADIX, mid * POS_RADIX, lo * POS_RADIX, hi, mid, lo)):
        qbias = qbias.at[:, 0, i].set(t)

    def digits(pos, n_rows):
        d = jnp.zeros((n_rows, LANES), F32)
        for i in range(3):
            d = d.at[:pos.shape[0], i].set((pos // POS_RADIX).astype(F32))
            d = d.at[:pos.shape[0], 3 + i].set((pos % POS_RADIX).astype(F32))
        return d.astype(BF16)

    posk = digits(n_meta + jnp.arange(seq_len), seq_len)
    poskm = digits(jnp.arange(n_meta), LANES)
    key_t = jnp.arange(tq)[:, None]
    qry = jnp.arange(2 * tq)[None, :] % tq
    dmask = jnp.where(key_t <= qry, 0.0, NEG_INF).astype(F32)
    mmask = jnp.broadcast_to(jnp.where(jnp.arange(LANES)[:, None] < n_meta, 0.0, NEG_INF), (LANES, 2 * tq)).astype(F32)
    return posk, poskm, qbias.astype(BF16), dmask, mmask


def _attn_prompt(q, kb, v, lamv, subln, n_seq, seq_len, n_meta, meta_row0, lam_init):
    tq = _row_tile(seq_len, 512)
    assert tq % LANES == 0
    nq = seq_len // tq
    hps = ATT_HEADS_PER_STEP
    wblk = hps * DA_VD
    meta_blk0 = meta_row0 // n_meta
    posk, poskm, qbias, dmask, mmask = _alibi_tables(seq_len, n_meta, tq)
    dv1 = DA_VD + ONES_ROWS
    body = functools.partial(_attn_prompt_body, tq=tq, lam_init=lam_init)
    const = lambda a: pl.BlockSpec(a.shape, lambda b, h, qi: (0,) * a.ndim, pipeline_mode=pl.Buffered(1))
    return pl.pallas_call(
        body,
        out_shape=jax.ShapeDtypeStruct((n_seq * seq_len, DA_HEADS * DA_VD), F32),
        grid=(n_seq, DA_HEADS // hps, nq),
        in_specs=[pl.BlockSpec((tq, wblk), lambda b, h, qi: (b * nq + qi, h)),
                  pl.BlockSpec((seq_len, wblk), lambda b, h, qi: (b, h)),
                  pl.BlockSpec((seq_len, wblk), lambda b, h, qi: (b, h)),
                  pl.BlockSpec((n_meta, wblk), lambda b, h, qi: (meta_blk0 + b, h)),
                  pl.BlockSpec((n_meta, wblk), lambda b, h, qi: (meta_blk0 + b, h)),
                  const(posk), const(poskm),
                  pl.BlockSpec((hps, 1, LANES), lambda b, h, qi: (h, 0, 0)),
                  const(dmask), const(mmask), const(lamv), const(subln)],
        out_specs=pl.BlockSpec((tq, wblk), lambda b, h, qi: (b * nq + qi, h)),
        scratch_shapes=[pltpu.VMEM((hps, dv1, seq_len), BF16), pltpu.VMEM((hps, dv1, LANES), BF16),
                        pltpu.VMEM((hps, 2 * tq, 2 * DA_VD), BF16), pltpu.VMEM((hps, 1, 2 * tq), F32),
                        pltpu.VMEM((hps, dv1, 2 * tq), F32)],
        compiler_params=_cparams(3),
        name="attn_prompt",
    )(q, kb, v, kb, v, posk, poskm, qbias, dmask, mmask, lamv, subln)


def _attn_meta_body(slopes_ref, q_ref, k_ref, v_ref, lamv_ref, subln_ref, o_ref,
                    m_ref, l_ref, acc_ref, *, n_meta, lam_init):
    h = pl.program_id(1)
    slope = slopes_ref[h]
    m_rows = 2 * n_meta
    m_ref[...] = jnp.full(m_ref.shape, NEG_INF, F32)
    l_ref[...] = jnp.zeros(l_ref.shape, F32)
    acc_ref[...] = jnp.zeros(acc_ref.shape, F32)
    qb = _split_maps(q_ref[...])
    r = lax.broadcasted_iota(jnp.int32, (m_rows, LANES), 0)
    r = jnp.where(r >= n_meta, r - n_meta, r)
    c = lax.broadcasted_iota(jnp.int32, (m_rows, LANES), 1)
    bias = jnp.where(c <= r, slope * c.astype(F32), NEG_INF)
    s = _dot_nt(qb, _pad_rows(k_ref[...], LANES).astype(BF16)) + bias
    _softmax_step([(s, _pad_rows(v_ref[...], LANES).astype(BF16))], m_ref, l_ref, acc_ref)
    on = acc_ref[...] / l_ref[...]
    o = on[:n_meta] - _lambda(lamv_ref, lam_init) * on[n_meta:]
    o_ref[...] = _rms(o, subln_ref[...], 1e-5) * (1.0 - lam_init)


def _attn_meta(q, k, v, slopes, lamv, subln, n_seq, n_meta, meta_row0, lam_init):
    blk0 = meta_row0 // n_meta
    spec = pl.BlockSpec((n_meta, DA_VD), lambda b, h: (blk0 + b, h))
    body = functools.partial(_attn_meta_body, n_meta=n_meta, lam_init=lam_init)
    return pl.pallas_call(
        body,
        out_shape=jax.ShapeDtypeStruct((n_seq * n_meta, DA_HEADS * DA_VD), F32),
        grid=(n_seq, DA_HEADS),
        in_specs=[pl.BlockSpec(memory_space=pltpu.SMEM), spec, spec, spec,
                  pl.BlockSpec(lamv.shape, lambda b, h: (0, 0)),
                  pl.BlockSpec(subln.shape, lambda b, h: (0, 0))],
        out_specs=pl.BlockSpec((n_meta, DA_VD), lambda b, h: (b, h)),
        scratch_shapes=[pltpu.VMEM((2 * n_meta, LANES), F32), pltpu.VMEM((2 * n_meta, LANES), F32),
                        pltpu.VMEM((2 * n_meta, DA_VD), F32)],
        compiler_params=_cparams(2),
        name="attn_meta",
    )(slopes, q, k, v, lamv, subln)


PAGES_PER_STEP = 8


def _attn_sample_body(pt_ref, q_ref, kn_ref, vn_ref, *rest, dec_len, n_pages, n_grp, lam_init):
    ck_refs, cv_refs = rest[:n_grp], rest[n_grp:2 * n_grp]
    base_ref, basen_ref, rslope_ref, lamv_ref, subln_ref, o_ref, qb_ref, m_ref, l_ref, acc_ref = rest[2 * n_grp:]
    p = pl.program_id(1)
    rows_per_head = 2 * dec_len

    @pl.when(p == 0)
    def _start():
        lane = lax.broadcasted_iota(jnp.int32, (dec_len, DA_VD), 1)
        for h in range(DA_HEADS):
            qh = q_ref[:, h * DA_VD:(h + 1) * DA_VD]
            qb_ref[h * rows_per_head:h * rows_per_head + dec_len, :] = jnp.where(lane < DA_DH, qh, 0.0)
            qb_ref[h * rows_per_head + dec_len:(h + 1) * rows_per_head, :] = jnp.where(lane >= DA_DH, qh, 0.0)
        m_ref[...] = jnp.full(m_ref.shape, NEG_INF, F32)
        l_ref[...] = jnp.zeros(l_ref.shape, F32)
        acc_ref[...] = jnp.zeros(acc_ref.shape, F32)

    qb = qb_ref[...]
    rslope = rslope_ref[...]
    shift = rslope * ((p * n_grp - n_pages) * PAGE).astype(F32)
    blocks = []
    for u in range(n_grp):
        s = _dot_nt(qb, ck_refs[u][0].astype(BF16)) + base_ref[...]
        if u:
            s = s + rslope[:, :1] * float(u * PAGE)
        blocks.append((s, cv_refs[u][0].astype(BF16)))
    _softmax_step(blocks, m_ref, l_ref, acc_ref, shift=shift)

    @pl.when(p == n_pages // n_grp - 1)
    def _finish():
        s = _dot_nt(qb, _pad_rows(kn_ref[0], LANES).astype(BF16)) + basen_ref[...]
        _softmax_step([(s, _pad_rows(vn_ref[0], LANES).astype(BF16))], m_ref, l_ref, acc_ref)
        on = acc_ref[...] / l_ref[...]
        lam = _lambda(lamv_ref, lam_init)
        for h in range(DA_HEADS):
            o = (on[h * rows_per_head:h * rows_per_head + dec_len]
                 - lam * on[h * rows_per_head + dec_len:(h + 1) * rows_per_head])
            o_ref[:, h * DA_VD:(h + 1) * DA_VD] = _rms(o, subln_ref[...], 1e-5) * (1.0 - lam_init)


def _attn_sample(q, kn, vn, ck, cv, page_table, lamv, subln, n_seq, dec_len, samp_row0, lam_init):
    n_pages = page_table.shape[1]
    n_grp = math.gcd(n_pages, PAGES_PER_STEP)
    rows = DA_HEADS * 2 * dec_len
    assert rows == LANES and DA_HEADS * dec_len <= LANES
    page_rows = PAGE * DA_HEADS
    slopes = 2.0 ** (-8.0 * (jnp.arange(rows) // (2 * dec_len) + 1) / DA_HEADS) * LOG2E
    rhead = jnp.arange(rows) // (2 * dec_len)
    rq = jnp.arange(rows) % dec_len
    cpos = jnp.arange(page_rows) // DA_HEADS
    chead = jnp.arange(page_rows) % DA_HEADS
    base = jnp.where(rhead[:, None] == chead[None, :], slopes[:, None] * cpos[None, :].astype(F32), NEG_INF)
    cn = jnp.arange(LANES)
    valid = (cn[None, :] < DA_HEADS * dec_len) & (rhead[:, None] == (cn % DA_HEADS)[None, :]) \
        & ((cn // DA_HEADS)[None, :] <= rq[:, None])
    basen = jnp.where(valid, slopes[:, None] * (cn // DA_HEADS)[None, :].astype(F32), NEG_INF)
    rslope = jnp.broadcast_to(slopes[:, None], (rows, LANES)).astype(F32)
    blk0 = samp_row0 // dec_len
    body = functools.partial(_attn_sample_body, dec_len=dec_len, n_pages=n_pages, n_grp=n_grp, lam_init=lam_init)
    width = DA_HEADS * DA_VD

    def page(u):
        return pl.BlockSpec((1, page_rows, DA_VD), lambda i, p, pt: (pt[i, p * n_grp + u], 0, 0))

    const = lambda a: pl.BlockSpec(a.shape, lambda i, p, pt: (0,) * a.ndim)
    grid_spec = pltpu.PrefetchScalarGridSpec(
        num_scalar_prefetch=1,
        grid=(n_seq, n_pages // n_grp),
        in_specs=[pl.BlockSpec((dec_len, width), lambda i, p, pt: (blk0 + i, 0)),
                  pl.BlockSpec((1,) + kn.shape[1:], lambda i, p, pt: (i, 0, 0)),
                  pl.BlockSpec((1,) + vn.shape[1:], lambda i, p, pt: (i, 0, 0))]
                 + [page(u) for u in range(n_grp)] * 2
                 + [const(base), const(basen), const(rslope), const(lamv), const(subln)],
        out_specs=pl.BlockSpec((dec_len, width), lambda i, p, pt: (i, 0)),
        scratch_shapes=[pltpu.VMEM((rows, DA_VD), BF16), pltpu.VMEM((rows, LANES), F32),
                        pltpu.VMEM((rows, LANES), F32), pltpu.VMEM((rows, DA_VD), F32)])
    return pl.pallas_call(
        body,
        out_shape=jax.ShapeDtypeStruct((n_seq * dec_len, width), F32),
        grid_spec=grid_spec,
        compiler_params=_cparams(2),
        name="attn_sample",
    )(page_table, q, kn, vn, *([ck] * n_grp), *([cv] * n_grp), base.astype(F32), basen.astype(F32), rslope,
      lamv, subln)


def kernel(x_prompt, x_sample, state_gla, state_ssd, state_conv, cache_k, cache_v, page_table, meta_tokens, norm_ffn1, w_ffn1_gu, w_ffn1_down, norm_mix, norm_ffn2, w_ffn2_gu, w_ffn2_down, w_in_ab, gla_gk_w2, gla_gk_b, gla_norm, ssd_conv_w, ssd_conv_b, ssd_dt_bias, ssd_a_log, ssd_d, ssd_norm, w_out_ab, w_qkv_c, lam_q1, lam_k1, lam_q2, lam_k2, subln_c, w_out_c, norm_final):
    n_b, seq, d = x_prompt.shape
    n_s, dec, _ = x_sample.shape
    n_meta = meta_tokens.shape[0]
    depth = norm_ffn1.shape[0]
    assert seq % CHUNK == 0 and n_meta % SUBLANES == 0 and dec % SUBLANES == 0
    main_rows, meta_rows, samp_rows = n_b * seq, n_b * n_meta, n_s * dec
    meta_row0, samp_row0 = main_rows, main_rows + meta_rows

    x = jnp.concatenate([x_prompt.reshape(main_rows, d),
                         jnp.tile(meta_tokens.astype(F32), (n_b, 1)),
                         x_sample.reshape(samp_rows, d)], axis=0)

    row = lambda a: a.reshape(1, -1).astype(F32)
    gla_p, gla_s, ssd_p, ssd_s, conv_p, conv_s = [], [], [], [], [], []
    k_p, v_p, k_s, v_s = [], [], [], []
    conv_dim = ssd_conv_w.shape[2]
    gqk, gv = GLA_HEADS * GLA_DK, GLA_HEADS * GLA_DV
    bounds = [0]
    for w in (gqk, gqk, gv, GLA_RANK, gv, SSD_INNER, conv_dim, SSD_HEADS):
        bounds.append(bounds[-1] + w)
    expand = (jnp.arange(SSD_INNER)[None, :] // SSD_HEADDIM == jnp.arange(SSD_HEADS)[:, None]).astype(BF16)

    for l in range(depth):
        x = _ffn(x, row(norm_ffn1[l]), *_ffn_weights(w_ffn1_gu[l], w_ffn1_down[l]), row(norm_final), False)
        if l % 2 == 0:
            e = l // 2
            ws = [w_in_ab[e][:, bounds[i]:bounds[i + 1]].astype(BF16) for i in range(8)]
            proj = _inproj_ab(x, row(norm_mix[l]), ws, gla_gk_w2[e].astype(BF16), row(gla_gk_b[e]),
                              row(ssd_dt_bias[e]))
            params = (row(gla_norm[e]), ssd_conv_w[e].astype(F32), row(ssd_conv_b[e]), row(ssd_a_log[e]),
                      row(jnp.repeat(ssd_d[e], SSD_HEADDIM)), row(ssd_norm[e]), expand)
            ssd_shape = (SSD_INNER, SSD_STATE)
            zg = jnp.zeros((n_b, GLA_HEADS, GLA_DK, GLA_DV), F32)
            zs = jnp.zeros((n_b,) + ssd_shape, F32)
            zc = jnp.zeros((n_b, SSD_CONV - 1, conv_dim), F32)
            mix_meta, sg, ss, cv = _scan(proj, meta_row0, n_b, n_meta, n_meta, zg, zs, zc, params)
            mix_main, sg, ss, cv = _scan(proj, 0, n_b, seq, CHUNK, sg, ss, cv, params)
            gla_p.append(sg)
            ssd_p.append(ss.reshape(n_b, SSD_HEADS, SSD_HEADDIM, SSD_STATE))
            conv_p.append(cv)
            mix_samp, sg, ss, cv = _scan(proj, samp_row0, n_s, dec, dec, state_gla[e].astype(F32),
                                         state_ssd[e].astype(F32).reshape((n_s,) + ssd_shape),
                                         state_conv[e].astype(F32), params)
            gla_s.append(sg)
            ssd_s.append(ss.reshape(n_s, SSD_HEADS, SSD_HEADDIM, SSD_STATE))
            conv_s.append(cv)
            w_out = w_out_ab[e].astype(BF16)
            for mix, row0 in ((mix_main, 0), (mix_meta, meta_row0), (mix_samp, samp_row0)):
                x = _outproj(x, mix, w_out, row0)
        else:
            o = l // 2
            lam_init = 0.8 - 0.6 * math.exp(-0.3 * l)
            width = DA_HEADS * DA_VD
            wq, wk, wv = (w_qkv_c[o][:, i * width:(i + 1) * width].astype(BF16) for i in range(3))
            q, k, v, kb = _qkv(x, row(norm_mix[l]), wq, wk, wv)
            slopes = jnp.exp2(-8.0 * jnp.arange(1, DA_HEADS + 1, dtype=F32) / DA_HEADS) * LOG2E
            lamv = jnp.stack([lam_q1[o], lam_k1[o], lam_q2[o], lam_k2[o]]).astype(F32)
            subln = row(subln_c[o])
            o_main = _attn_prompt(q, kb, v, lamv, subln, n_b, seq, n_meta, meta_row0, lam_init)
            o_meta = _attn_meta(q, k, v, slopes, lamv, subln, n_b, n_meta, meta_row0, lam_init)
            kn = k[samp_row0:].reshape(n_s, dec * DA_HEADS, DA_VD)
            vn = v[samp_row0:].reshape(n_s, dec * DA_HEADS, DA_VD)
            ck = cache_k[o].reshape(cache_k.shape[1], PAGE * DA_HEADS, DA_VD)
            cv_ = cache_v[o].reshape(cache_v.shape[1], PAGE * DA_HEADS, DA_VD)
            o_samp = _attn_sample(q, kn, vn, ck, cv_, page_table, lamv, subln, n_s, dec, samp_row0, lam_init)
            w_out = w_out_c[o].astype(BF16)
            for att, row0 in ((o_main, 0), (o_meta, meta_row0), (o_samp, samp_row0)):
                x = _outproj(x, att, w_out, row0)
            rows5 = lambda a, n, t: a.reshape(n, t, DA_HEADS, DA_VD)
            k_p.append(jnp.concatenate([rows5(k[meta_row0:samp_row0], n_b, n_meta), rows5(k[:main_rows], n_b, seq)], axis=1))
            v_p.append(jnp.concatenate([rows5(v[meta_row0:samp_row0], n_b, n_meta), rows5(v[:main_rows], n_b, seq)], axis=1))
            k_s.append(rows5(k[samp_row0:], n_s, dec))
            v_s.append(rows5(v[samp_row0:], n_s, dec))
        x = _ffn(x, row(norm_ffn2[l]), *_ffn_weights(w_ffn2_gu[l], w_ffn2_down[l]), row(norm_final),
                 l == depth - 1)

    y_prompt = x[:main_rows].reshape(n_b, seq, d)
    y_sample = x[samp_row0:].reshape(n_s, dec, d)
    return (y_prompt, y_sample, jnp.stack(gla_p), jnp.stack(gla_s), jnp.stack(ssd_p), jnp.stack(ssd_s),
            jnp.stack(conv_p), jnp.stack(conv_s), jnp.stack(k_p), jnp.stack(v_p), jnp.stack(k_s), jnp.stack(v_s))
```

```python
import functools
import math

import jax
import jax.numpy as jnp
from jax import lax
from jax.experimental import pallas as pl
from jax.experimental.pallas import tpu as pltpu

F32 = jnp.float32
BF16 = jnp.bfloat16

CHUNK = 64
GLA_HEADS, GLA_DK, GLA_DV, GLA_RANK = 4, 128, 256, 16
GLA_GATE_NORM = 16.0
SSD_HEADS, SSD_HEADDIM, SSD_STATE, SSD_GROUPS, SSD_CONV = 16, 64, 64, 2, 4
SSD_INNER = SSD_HEADS * SSD_HEADDIM
DA_HEADS, DA_DH = 8, 64
DA_VD = 2 * DA_DH
PAGE = 128

LANES = 128
SUBLANES = 8
VMEM_LIMIT = 56 * 1024 * 1024

NEG_INF = float("-inf")


def _cparams(n_axes):
    return pltpu.CompilerParams(dimension_semantics=("arbitrary",) * n_axes,
                                vmem_limit_bytes=VMEM_LIMIT)


def _row_tile(n, target):
    best = SUBLANES
    for t in range(SUBLANES, target + 1, SUBLANES):
        if n % t == 0:
            best = t
    return best


def _resident(shape):
    nd = len(shape)
    return pl.BlockSpec(shape, lambda *_: (0,) * nd, pipeline_mode=pl.Buffered(1))


def _rms(x, w, eps):
    ms = jnp.mean(x * x, axis=-1, keepdims=True)
    return x * lax.rsqrt(ms + eps) * w


def _silu(x):
    return x * jax.nn.sigmoid(x)


def _softplus(x):
    return jnp.maximum(x, 0.0) + jnp.log1p(jnp.exp(-jnp.abs(x)))


def _dot(a, b):
    return jnp.dot(a, b, preferred_element_type=F32)


def _dot_nt(a, b):
    return lax.dot_general(a, b, (((1,), (1,)), ((), ())), preferred_element_type=F32)


def _dot_tn(a, b):
    return lax.dot_general(a, b, (((0,), (0,)), ((), ())), preferred_element_type=F32)


def _split3(a):
    hi = a.astype(BF16)
    r1 = a - hi.astype(F32)
    mid = r1.astype(BF16)
    lo = (r1 - mid.astype(F32)).astype(BF16)
    return hi, mid, lo


def _dot01_left(m01, a):
    hi, mid, lo = _split3(a)
    return _dot(m01, hi) + _dot(m01, mid) + _dot(m01, lo)


def _dot01_right(a, m01):
    hi, mid, lo = _split3(a)
    return _dot(hi, m01) + _dot(mid, m01) + _dot(lo, m01)


FFN_CHUNK = 256


def _ffn_body(x_ref, nw_ref, wg_ref, wu_ref, wd_ref, fw_ref, o_ref, *, n_chunks, final_norm):
    x = x_ref[...]
    xn = _rms(x, nw_ref[...], 1e-6).astype(BF16)
    acc = None
    for c in range(n_chunks):
        g = _dot(xn, wg_ref[c])
        u = _dot(xn, wu_ref[c])
        down = _dot((_silu(g) * u).astype(BF16), wd_ref[c])
        acc = down if acc is None else acc + down
    y = x + 0.5 * acc
    if final_norm:
        y = _rms(y, fw_ref[...], 1e-6)
    o_ref[...] = y


def _ffn(x, nw, wg, wu, wd, fw, final_norm, row0=0, n_rows=None, tile=896):
    n, d = x.shape
    n_rows = n if n_rows is None else n_rows
    n_chunks = wg.shape[0]
    tm = _row_tile(math.gcd(n_rows, row0) if row0 else n_rows, tile)
    blk0 = row0 // tm
    body = functools.partial(_ffn_body, n_chunks=n_chunks, final_norm=final_norm)
    return pl.pallas_call(
        body,
        out_shape=jax.ShapeDtypeStruct((n_rows, d), F32),
        grid=(n_rows // tm,),
        in_specs=[pl.BlockSpec((tm, d), lambda i: (blk0 + i, 0)),
                  _resident(nw.shape), _resident(wg.shape), _resident(wu.shape),
                  _resident(wd.shape), _resident(fw.shape)],
        out_specs=pl.BlockSpec((tm, d), lambda i: (i, 0)),
        compiler_params=_cparams(1),
        name="ffn",
    )(x, nw, wg, wu, wd, fw)


def _ffn_weights(w_gu, w_down):
    d, two_ff = w_gu.shape
    ff = two_ff // 2
    nc = ff // FFN_CHUNK
    wg = w_gu[:, :ff].reshape(d, nc, FFN_CHUNK).transpose(1, 0, 2).astype(BF16)
    wu = w_gu[:, ff:].reshape(d, nc, FFN_CHUNK).transpose(1, 0, 2).astype(BF16)
    wd = w_down.reshape(nc, FFN_CHUNK, d).astype(BF16)
    return wg, wu, wd


def _inproj_ab_body(x_ref, nw_ref, wq_ref, wk_ref, wv_ref, wgk_ref, wg_ref, wz_ref, wx_ref, wdt_ref,
                    gk2_ref, gkb_ref, dtb_ref,
                    q_ref, k_ref, v_ref, lf_ref, g_ref, z_ref, xbc_ref, dt_ref):
    xn = _rms(x_ref[...], nw_ref[...], 1e-6).astype(BF16)
    q_ref[...] = _dot(xn, wq_ref[...]) * (GLA_DK ** -0.5)
    k_ref[...] = _dot(xn, wk_ref[...])
    v_ref[...] = _dot(xn, wv_ref[...])
    gk_lo = _dot(xn, wgk_ref[...]).astype(BF16)
    gk = _dot(gk_lo, gk2_ref[...]) + gkb_ref[...]
    lf_ref[...] = -_softplus(-gk) / GLA_GATE_NORM
    g_ref[...] = _dot(xn, wg_ref[...])
    z_ref[...] = _dot(xn, wz_ref[...])
    xbc_ref[...] = _dot(xn, wx_ref[...])
    dt_ref[...] = _softplus(_dot(xn, wdt_ref[...]) + dtb_ref[...])


def _inproj_ab(x, nw, ws, gk2, gkb, dtb):
    n, d = x.shape
    tm = _row_tile(n, 448)
    widths = [w.shape[1] for w in ws]
    out_w = [widths[0], widths[1], widths[2], gk2.shape[1], widths[4], widths[5], widths[6], widths[7]]
    return pl.pallas_call(
        _inproj_ab_body,
        out_shape=[jax.ShapeDtypeStruct((n, w), F32) for w in out_w],
        grid=(n // tm,),
        in_specs=[pl.BlockSpec((tm, d), lambda i: (i, 0)), _resident(nw.shape)]
                 + [_resident(w.shape) for w in ws]
                 + [_resident(gk2.shape), _resident(gkb.shape), _resident(dtb.shape)],
        out_specs=[pl.BlockSpec((tm, w), lambda i: (i, 0)) for w in out_w],
        compiler_params=_cparams(1),
        name="inproj_ab",
    )(x, nw, *ws, gk2, gkb, dtb)


LOG2E = 1.4426950408889634


def _qkv_body(x_ref, nw_ref, wq_ref, wk_ref, wv_ref, q_ref, k_ref, v_ref, kb_ref):
    xn = _rms(x_ref[...], nw_ref[...], 1e-6).astype(BF16)
    q_ref[...] = (_dot(xn, wq_ref[...]) * (DA_DH ** -0.5 * LOG2E)).astype(BF16)
    k = _dot(xn, wk_ref[...])
    k_ref[...] = k
    kb_ref[...] = k.astype(BF16)
    v_ref[...] = _dot(xn, wv_ref[...])


def _qkv(x, nw, wq, wk, wv):
    n, d = x.shape
    tm = _row_tile(n, 896)
    w = wq.shape[1]
    return pl.pallas_call(
        _qkv_body,
        out_shape=[jax.ShapeDtypeStruct((n, w), BF16), jax.ShapeDtypeStruct((n, w), F32),
                   jax.ShapeDtypeStruct((n, w), F32), jax.ShapeDtypeStruct((n, w), BF16)],
        grid=(n // tm,),
        in_specs=[pl.BlockSpec((tm, d), lambda i: (i, 0)), _resident(nw.shape),
                  _resident(wq.shape), _resident(wk.shape), _resident(wv.shape)],
        out_specs=[pl.BlockSpec((tm, w), lambda i: (i, 0))] * 4,
        compiler_params=_cparams(1),
        name="qkv",
    )(x, nw, wq, wk, wv)


def _outproj_body(x_ref, m_ref, w_ref, o_ref):
    o_ref[...] = x_ref[...] + _dot(m_ref[...].astype(BF16), w_ref[...])


def _outproj(x, mix, w, row0):
    n, d = x.shape
    ns, km = mix.shape
    tm = _row_tile(math.gcd(ns, row0) if row0 else ns, 1024)
    blk0 = row0 // tm
    return pl.pallas_call(
        _outproj_body,
        out_shape=jax.ShapeDtypeStruct((n, d), F32),
        grid=(ns // tm,),
        in_specs=[pl.BlockSpec((tm, d), lambda i: (blk0 + i, 0)), pl.BlockSpec((tm, km), lambda i: (i, 0)),
                  _resident(w.shape)],
        out_specs=pl.BlockSpec((tm, d), lambda i: (blk0 + i, 0)),
        input_output_aliases={0: 0},
        compiler_params=_cparams(1),
        name="outproj",
    )(x, mix, w)


HIST = SUBLANES


SEQS_PER_STEP = 2


def _scan_chunk(q_ref, k_ref, v_ref, lf_ref, g_ref, z_ref, xbc_ref, dt_ref,
                gnorm_ref, cw_ref, cb_ref, alog_ref, dskip_ref, snorm_ref, expand_ref,
                mix_ref, st_ref, ss_ref, xpad_ref, y_ref, *, c_len):
    kc = SSD_CONV - 1

    row = lax.broadcasted_iota(jnp.int32, (c_len, c_len), 0)
    col = lax.broadcasted_iota(jnp.int32, (c_len, c_len), 1)
    tril = row >= col
    ltri = jnp.where(tril, 1.0, 0.0).astype(BF16)

    xpad_ref[HIST:HIST + c_len, :] = xbc_ref[...]
    acc = cb_ref[...] + xpad_ref[HIST - kc:HIST - kc + c_len, :] * cw_ref[0:1, :]
    for t in range(1, SSD_CONV):
        acc = acc + xpad_ref[HIST - kc + t:HIST - kc + t + c_len, :] * cw_ref[t:t + 1, :]
    tail = xpad_ref[HIST + c_len - kc:HIST + c_len, :]
    xpad_ref[HIST - kc:HIST, :] = tail
    xc = _silu(acc)
    xs = xc[:, :SSD_INNER]
    gs = SSD_GROUPS * SSD_STATE
    bmat = xc[:, SSD_INNER:SSD_INNER + gs]
    cmat = xc[:, SSD_INNER + gs:SSD_INNER + 2 * gs]

    expand = expand_ref[...]
    dt = dt_ref[...]
    da = dt * (-jnp.exp(alog_ref[...]))
    cs = _dot01_left(ltri, da)
    cs_t = cs.T
    cs_x = _dot01_right(cs, expand)
    dt_x = _dot01_right(dt, expand)
    cs_last_x = cs_x[c_len - 1:c_len, :]
    xdt = xs * dt_x
    e_cs = jnp.exp(cs_x)
    xw = xdt * jnp.exp(cs_last_x - cs_x)
    g_last = jnp.exp(cs_last_x)
    hpg = SSD_HEADS // SSD_GROUPS
    gw = hpg * SSD_HEADDIM
    for gi in range(SSD_GROUPS):
        bg = bmat[:, gi * SSD_STATE:(gi + 1) * SSD_STATE].astype(BF16)
        cg = cmat[:, gi * SSD_STATE:(gi + 1) * SSD_STATE].astype(BF16)
        gmat = _dot_nt(cg, bg)
        s_g = ss_ref[gi * gw:(gi + 1) * gw, :]
        y_inter = _dot_nt(cg, s_g.astype(BF16)) * e_cs[:, gi * gw:(gi + 1) * gw]
        ds_g = _dot_tn(xw[:, gi * gw:(gi + 1) * gw].astype(BF16), bg)
        for hh in range(hpg):
            h = gi * hpg + hh
            lo, hi = h * SSD_HEADDIM, (h + 1) * SSD_HEADDIM
            seg = jnp.broadcast_to(cs[:, h:h + 1], (c_len, c_len)) - cs_t[h:h + 1, :]
            decay = jnp.exp(jnp.where(tril, seg, NEG_INF))
            sc = (gmat * decay).astype(BF16)
            y_intra = _dot(sc, xdt[:, lo:hi].astype(BF16))
            y_ref[:, lo:hi] = y_intra + y_inter[:, hh * SSD_HEADDIM:(hh + 1) * SSD_HEADDIM]
            ss_ref[lo:hi, :] = (g_last[:, lo:hi] * s_g[hh * SSD_HEADDIM:(hh + 1) * SSD_HEADDIM, :]
                                + ds_g[hh * SSD_HEADDIM:(hh + 1) * SSD_HEADDIM, :])
    y = (y_ref[...] + xs * dskip_ref[...]) * _silu(z_ref[...])
    half = SSD_INNER // SSD_GROUPS
    for gi in range(SSD_GROUPS):
        mix_ref[:, GLA_HEADS * GLA_DV + gi * half:GLA_HEADS * GLA_DV + (gi + 1) * half] = _rms(
            y[:, gi * half:(gi + 1) * half], snorm_ref[:, gi * half:(gi + 1) * half], 1e-6)

    b = _dot01_left(ltri, lf_ref[...])
    b_last = b[c_len - 1:c_len, :]
    q_dec = q_ref[...] * jnp.exp(b)
    k = k_ref[...]
    k_inv = k * jnp.exp(-b)
    k_end = k * jnp.exp(b_last - b)
    g_row = jnp.exp(b_last)
    for h in range(GLA_HEADS):
        ks, ke = h * GLA_DK, (h + 1) * GLA_DK
        vs, ve = h * GLA_DV, (h + 1) * GLA_DV
        qd = q_dec[:, ks:ke].astype(BF16)
        att = jnp.where(tril, _dot_nt(qd, k_inv[:, ks:ke].astype(BF16)), 0.0)
        v_h = v_ref[:, vs:ve].astype(BF16)
        st = st_ref[h]
        o = _dot(att.astype(BF16), v_h) + _dot_nt(qd, st.astype(BF16))
        st_ref[h] = g_row[:, ks:ke] * st + _dot_tn(v_h, k_end[:, ks:ke].astype(BF16))
        mix_ref[:, vs:ve] = _rms(o, gnorm_ref[...], 1e-6) * _silu(g_ref[:, vs:ve])


N_PROJ = 8
N_SCAN_PARAMS = 7


def _scan_body(*refs, c_len, n_steps, sps):
    proj = refs[:N_PROJ * sps]
    gla0_ref, ssd0_ref, conv0_ref = refs[N_PROJ * sps:N_PROJ * sps + 3]
    params = refs[N_PROJ * sps + 3:N_PROJ * sps + 3 + N_SCAN_PARAMS]
    mix_ref, gla_out_ref, ssd_out_ref, conv_out_ref, st_ref, ss_ref, xpad_ref, y_ref = \
        refs[N_PROJ * sps + 3 + N_SCAN_PARAMS:]
    j = pl.program_id(1)
    kc = SSD_CONV - 1
    conv_dim = xpad_ref.shape[2]

    @pl.when(j == 0)
    def _load_state():
        for s in range(sps):
            for h in range(GLA_HEADS):
                st_ref[s * GLA_HEADS + h] = gla0_ref[s, h].T
            ss_ref[s] = ssd0_ref[s]
            xpad_ref[s, 0:HIST, :] = jnp.zeros((HIST, conv_dim), F32)
            xpad_ref[s, HIST - kc:HIST, :] = conv0_ref[s]

    for s in range(sps):
        _scan_chunk(*proj[N_PROJ * s:N_PROJ * (s + 1)], *params, mix_ref.at[s],
                    st_ref.at[pl.ds(s * GLA_HEADS, GLA_HEADS)], ss_ref.at[s], xpad_ref.at[s], y_ref.at[s],
                    c_len=c_len)

    @pl.when(j == n_steps - 1)
    def _store_state():
        for s in range(sps):
            for h in range(GLA_HEADS):
                gla_out_ref[s, h] = st_ref[s * GLA_HEADS + h].T
            ssd_out_ref[s] = ss_ref[s]
            conv_out_ref[s] = xpad_ref[s, HIST - kc:HIST, :]


def _scan(proj, row0, n_seq, seq_len, c_len, gla0, ssd0, conv0, params):
    assert len(proj) == N_PROJ and len(params) == N_SCAN_PARAMS
    n_steps = seq_len // c_len
    blk0 = row0 // c_len
    conv_dim = proj[6].shape[1]
    mix_w = GLA_HEADS * GLA_DV + SSD_INNER
    sps = math.gcd(n_seq, SEQS_PER_STEP)

    def rows(w, s):
        return pl.BlockSpec((c_len, w), lambda b, j: (blk0 + (b * sps + s) * n_steps + j, 0))

    def per_seq(shape):
        nd = len(shape)
        return pl.BlockSpec((sps,) + tuple(shape[1:]), lambda b, j: (b,) + (0,) * (nd - 1))

    body = functools.partial(_scan_body, c_len=c_len, n_steps=n_steps, sps=sps)
    mix, gla, ssd, conv = pl.pallas_call(
        body,
        out_shape=[jax.ShapeDtypeStruct((n_seq, seq_len, mix_w), F32),
                   jax.ShapeDtypeStruct(gla0.shape, F32),
                   jax.ShapeDtypeStruct(ssd0.shape, F32),
                   jax.ShapeDtypeStruct(conv0.shape, F32)],
        grid=(n_seq // sps, n_steps),
        in_specs=[rows(a.shape[1], s) for s in range(sps) for a in proj]
                 + [per_seq(gla0.shape), per_seq(ssd0.shape), per_seq(conv0.shape)]
                 + [_resident(p.shape) for p in params],
        out_specs=[pl.BlockSpec((sps, c_len, mix_w), lambda b, j: (b, j, 0)),
                   per_seq(gla0.shape), per_seq(ssd0.shape), per_seq(conv0.shape)],
        scratch_shapes=[pltpu.VMEM((sps * GLA_HEADS, GLA_DV, GLA_DK), F32),
                        pltpu.VMEM((sps, SSD_INNER, SSD_STATE), F32),
                        pltpu.VMEM((sps, HIST + c_len, conv_dim), F32),
                        pltpu.VMEM((sps, c_len, SSD_INNER), F32)],
        compiler_params=_cparams(2),
        name=f"scan_c{c_len}",
    )(*(list(proj) * sps), gla0, ssd0, conv0, *params)
    return mix.reshape(n_seq * seq_len, mix_w), gla, ssd, conv


def _softmax_step(blocks, m_ref, l_ref, acc_ref, shift=None):
    m_prev = m_ref[...]
    s_max = None
    for s, _ in blocks:
        bm = jnp.max(s, axis=1, keepdims=True)
        s_max = bm if s_max is None else jnp.maximum(s_max, bm)
    if shift is None:
        m_new = jnp.maximum(m_prev, s_max)
        off = m_new
    else:
        m_new = jnp.maximum(m_prev, s_max + shift)
        off = m_new - shift
    alpha = jnp.exp2(m_prev - m_new)
    l = alpha * l_ref[...]
    acc = alpha * acc_ref[...]
    for s, v in blocks:
        p = jnp.exp2(s - off[:, :1])
        l = l + jnp.sum(p, axis=1, keepdims=True)
        acc = acc + _dot(p.astype(BF16), v)
    l_ref[...] = l
    acc_ref[...] = acc
    m_ref[...] = m_new


def _softmax_step_t(blocks, m_ref, acc_ref, hh):
    m_prev = m_ref[hh]
    m_new = m_prev
    for s, _ in blocks:
        m_new = jnp.maximum(m_new, jnp.max(s, axis=0, keepdims=True))
    acc = jnp.exp2(m_prev - m_new) * acc_ref[hh]
    for s, vt1 in blocks:
        acc = acc + _dot(vt1, jnp.exp2(s - m_new).astype(BF16))
    acc_ref[hh] = acc
    m_ref[hh] = m_new


def _lambda(lamv_ref, lam_init):
    lv = lamv_ref[...]
    s1 = jnp.sum(lv[0:1, :] * lv[1:2, :], axis=1, keepdims=True)
    s2 = jnp.sum(lv[2:3, :] * lv[3:4, :], axis=1, keepdims=True)
    return jnp.exp(s1) - jnp.exp(s2) + lam_init


def _pad_rows(x, n):
    return jnp.concatenate([x, jnp.zeros((n - x.shape[0], x.shape[1]), x.dtype)], axis=0)


def _split_maps(qv):
    lane = lax.broadcasted_iota(jnp.int32, qv.shape, 1)
    return jnp.concatenate([jnp.where(lane < DA_DH, qv, 0.0), jnp.where(lane >= DA_DH, qv, 0.0)], axis=0)


ATT_HEADS_PER_STEP = 2
ONES_ROWS = 16
POS_RADIX = 64


def _attn_prompt_body(q_ref, k_ref, v_ref, km_ref, vm_ref, posk_ref, poskm_ref, qbias_ref, dmask_ref, mmask_ref,
                      lamv_ref, subln_ref, o_ref, vt_ref, vtm_ref, q2_ref, m_ref, acc_ref, s_ref, *, tq, lam_init):
    qi = pl.program_id(2)
    seq = k_ref.shape[0]
    dv1 = DA_VD + ONES_ROWS

    @pl.when(qi == 0)
    def _transpose_v():
        for hh in range(ATT_HEADS_PER_STEP):
            lanes = slice(hh * DA_VD, (hh + 1) * DA_VD)
            for c in range(seq // tq):
                vt_ref[hh, 0:DA_VD, c * tq:(c + 1) * tq] = v_ref[c * tq:(c + 1) * tq, lanes].T.astype(BF16)
            vt_ref[hh, DA_VD:dv1, :] = jnp.ones((ONES_ROWS, seq), BF16)
            vtm_ref[hh, 0:DA_VD, :] = _pad_rows(vm_ref[:, lanes], LANES).T.astype(BF16)
            vtm_ref[hh, DA_VD:dv1, :] = jnp.ones((ONES_ROWS, LANES), BF16)

    for hh in range(ATT_HEADS_PER_STEP):
        lanes = slice(hh * DA_VD, (hh + 1) * DA_VD)
        q2_ref[hh, :, 0:DA_VD] = _split_maps(q_ref[:, lanes])
        q2_ref[hh, :, DA_VD:2 * DA_VD] = jnp.broadcast_to(qbias_ref[hh], (2 * tq, LANES))
        m_ref[hh] = jnp.full((1, 2 * tq), NEG_INF, F32)
        acc_ref[hh] = jnp.zeros((dv1, 2 * tq), F32)

    def keys(r0, hh):
        return jnp.concatenate([k_ref[pl.ds(r0, tq), hh * DA_VD:(hh + 1) * DA_VD], posk_ref[pl.ds(r0, tq), :]],
                               axis=1)

    def produce(j, slot):
        r0 = pl.multiple_of(j * tq, tq)
        for hh in range(ATT_HEADS_PER_STEP):
            s_ref[slot, hh] = _dot_nt(keys(r0, hh), q2_ref[hh])

    def consume(j, slot):
        r0 = pl.multiple_of(j * tq, tq)
        for hh in range(ATT_HEADS_PER_STEP):
            _softmax_step_t([(s_ref[slot, hh], vt_ref[hh, :, pl.ds(r0, tq)])], m_ref, acc_ref, hh)

    produce(0, 0)

    def two_blocks(t, carry):
        j = 2 * t
        consume(j, 0)
        produce(j + 1, 1)
        consume(j + 1, 1)
        produce(j + 2, 0)
        return carry

    lax.fori_loop(0, qi // 2, two_blocks, 0)

    @pl.when(qi % 2 == 1)
    def _odd_block():
        consume(qi - 1, 0)
        produce(qi, 1)

    r0 = pl.multiple_of(qi * tq, tq)
    lam = _lambda(lamv_ref, lam_init)
    for hh in range(ATT_HEADS_PER_STEP):
        lanes = slice(hh * DA_VD, (hh + 1) * DA_VD)
        s_d = s_ref[qi % 2, hh] + dmask_ref[...]
        kpm = jnp.concatenate([_pad_rows(km_ref[:, lanes], LANES), poskm_ref[...]], axis=1)
        s_m = _dot_nt(kpm, q2_ref[hh]) + mmask_ref[...]
        _softmax_step_t([(s_d, vt_ref[hh, :, pl.ds(r0, tq)]), (s_m, vtm_ref[hh])], m_ref, acc_ref, hh)
        a = acc_ref[hh]
        on = a[0:DA_VD] / a[DA_VD:DA_VD + 1]
        o_t = on[:, :tq] - lam * on[:, tq:]
        o_ref[:, lanes] = _rms(o_t.T, subln_ref[...], 1e-5) * (1.0 - lam_init)


def _alibi_tables(seq_len, n_meta, tq):
    assert seq_len + n_meta <= POS_RADIX * 256
    slopes = jnp.exp2(-8.0 * jnp.arange(1, DA_HEADS + 1, dtype=F32) / DA_HEADS) * LOG2E
    hi, mid, lo = (t.astype(F32) for t in _split3(slopes))
    qbias = jnp.zeros((DA_HEADS, 1, LANES), F32)
    for i, t in enumerate((hi * POS_RADIX, mid * POS_RADIX, lo * POS_RADIX, hi, mid, lo)):
        qbias = qbias.at[:, 0, i].set(t)

    def digits(pos, n_rows):
        d = jnp.zeros((n_rows, LANES), F32)
        for i in range(3):
            d = d.at[:pos.shape[0], i].set((pos // POS_RADIX).astype(F32))
            d = d.at[:pos.shape[0], 3 + i].set((pos % POS_RADIX).astype(F32))
        return d.astype(BF16)

    posk = digits(n_meta + jnp.arange(seq_len), seq_len)
    poskm = digits(jnp.arange(n_meta), LANES)
    key_t = jnp.arange(tq)[:, None]
    qry = jnp.arange(2 * tq)[None, :] % tq
    dmask = jnp.where(key_t <= qry, 0.0, NEG_INF).astype(F32)
    mmask = jnp.broadcast_to(jnp.where(jnp.arange(LANES)[:, None] < n_meta, 0.0, NEG_INF), (LANES, 2 * tq)).astype(F32)
    return posk, poskm, qbias.astype(BF16), dmask, mmask


def _attn_prompt(q, kb, v, lamv, subln, n_seq, seq_len, n_meta, meta_row0, lam_init):
    tq = _row_tile(seq_len, 512)
    assert tq % LANES == 0
    nq = seq_len // tq
    hps = ATT_HEADS_PER_STEP
    wblk = hps * DA_VD
    meta_blk0 = meta_row0 // n_meta
    posk, poskm, qbias, dmask, mmask = _alibi_tables(seq_len, n_meta, tq)
    dv1 = DA_VD + ONES_ROWS
    body = functools.partial(_attn_prompt_body, tq=tq, lam_init=lam_init)
    const = lambda a: pl.BlockSpec(a.shape, lambda b, h, qi: (0,) * a.ndim, pipeline_mode=pl.Buffered(1))
    return pl.pallas_call(
        body,
        out_shape=jax.ShapeDtypeStruct((n_seq * seq_len, DA_HEADS * DA_VD), F32),
        grid=(n_seq, DA_HEADS // hps, nq),
        in_specs=[pl.BlockSpec((tq, wblk), lambda b, h, qi: (b * nq + qi, h)),
                  pl.BlockSpec((seq_len, wblk), lambda b, h, qi: (b, h)),
                  pl.BlockSpec((seq_len, wblk), lambda b, h, qi: (b, h)),
                  pl.BlockSpec((n_meta, wblk), lambda b, h, qi: (meta_blk0 + b, h)),
                  pl.BlockSpec((n_meta, wblk), lambda b, h, qi: (meta_blk0 + b, h)),
                  const(posk), const(poskm),
                  pl.BlockSpec((hps, 1, LANES), lambda b, h, qi: (h, 0, 0)),
                  const(dmask), const(mmask), const(lamv), const(subln)],
        out_specs=pl.BlockSpec((tq, wblk), lambda b, h, qi: (b * nq + qi, h)),
        scratch_shapes=[pltpu.VMEM((hps, dv1, seq_len), BF16), pltpu.VMEM((hps, dv1, LANES), BF16),
                        pltpu.VMEM((hps, 2 * tq, 2 * DA_VD), BF16), pltpu.VMEM((hps, 1, 2 * tq), F32),
                        pltpu.VMEM((hps, dv1, 2 * tq), F32), pltpu.VMEM((2, hps, tq, 2 * tq), F32)],
        compiler_params=_cparams(3),
        name="attn_prompt",
    )(q, kb, v, kb, v, posk, poskm, qbias, dmask, mmask, lamv, subln)


def _attn_meta_body(slopes_ref, q_ref, k_ref, v_ref, lamv_ref, subln_ref, o_ref,
                    m_ref, l_ref, acc_ref, *, n_meta, lam_init):
    h = pl.program_id(1)
    slope = slopes_ref[h]
    m_rows = 2 * n_meta
    m_ref[...] = jnp.full(m_ref.shape, NEG_INF, F32)
    l_ref[...] = jnp.zeros(l_ref.shape, F32)
    acc_ref[...] = jnp.zeros(acc_ref.shape, F32)
    qb = _split_maps(q_ref[...])
    r = lax.broadcasted_iota(jnp.int32, (m_rows, LANES), 0)
    r = jnp.where(r >= n_meta, r - n_meta, r)
    c = lax.broadcasted_iota(jnp.int32, (m_rows, LANES), 1)
    bias = jnp.where(c <= r, slope * c.astype(F32), NEG_INF)
    s = _dot_nt(qb, _pad_rows(k_ref[...], LANES).astype(BF16)) + bias
    _softmax_step([(s, _pad_rows(v_ref[...], LANES).astype(BF16))], m_ref, l_ref, acc_ref)
    on = acc_ref[...] / l_ref[...]
    o = on[:n_meta] - _lambda(lamv_ref, lam_init) * on[n_meta:]
    o_ref[...] = _rms(o, subln_ref[...], 1e-5) * (1.0 - lam_init)


def _attn_meta(q, k, v, slopes, lamv, subln, n_seq, n_meta, meta_row0, lam_init):
    blk0 = meta_row0 // n_meta
    spec = pl.BlockSpec((n_meta, DA_VD), lambda b, h: (blk0 + b, h))
    body = functools.partial(_attn_meta_body, n_meta=n_meta, lam_init=lam_init)
    return pl.pallas_call(
        body,
        out_shape=jax.ShapeDtypeStruct((n_seq * n_meta, DA_HEADS * DA_VD), F32),
        grid=(n_seq, DA_HEADS),
        in_specs=[pl.BlockSpec(memory_space=pltpu.SMEM), spec, spec, spec,
                  pl.BlockSpec(lamv.shape, lambda b, h: (0, 0)),
                  pl.BlockSpec(subln.shape, lambda b, h: (0, 0))],
        out_specs=pl.BlockSpec((n_meta, DA_VD), lambda b, h: (b, h)),
        scratch_shapes=[pltpu.VMEM((2 * n_meta, LANES), F32), pltpu.VMEM((2 * n_meta, LANES), F32),
                        pltpu.VMEM((2 * n_meta, DA_VD), F32)],
        compiler_params=_cparams(2),
        name="attn_meta",
    )(slopes, q, k, v, lamv, subln)


PAGES_PER_STEP = 8


def _attn_sample_body(pt_ref, q_ref, kn_ref, vn_ref, *rest, dec_len, n_pages, n_grp, lam_init):
    ck_refs, cv_refs = rest[:n_grp], rest[n_grp:2 * n_grp]
    (base_ref, basen_ref, rslope_ref, lamv_ref, subln_ref, o_ref,
     qb_ref, m_ref, l_ref, acc_ref, s_ref) = rest[2 * n_grp:]
    p = pl.program_id(1)
    rows_per_head = 2 * dec_len

    @pl.when(p == 0)
    def _start():
        lane = lax.broadcasted_iota(jnp.int32, (dec_len, DA_VD), 1)
        for h in range(DA_HEADS):
            qh = q_ref[:, h * DA_VD:(h + 1) * DA_VD]
            qb_ref[h * rows_per_head:h * rows_per_head + dec_len, :] = jnp.where(lane < DA_DH, qh, 0.0)
            qb_ref[h * rows_per_head + dec_len:(h + 1) * rows_per_head, :] = jnp.where(lane >= DA_DH, qh, 0.0)
        m_ref[...] = jnp.full(m_ref.shape, NEG_INF, F32)
        l_ref[...] = jnp.zeros(l_ref.shape, F32)
        acc_ref[...] = jnp.zeros(acc_ref.shape, F32)

    def head_rows(ref, h):
        return ref[0, pl.ds(h, PAGE, stride=DA_HEADS), :].astype(BF16)

    for u in range(n_grp):
        for h in range(DA_HEADS):
            hr = slice(h * rows_per_head, (h + 1) * rows_per_head)
            s_ref[hr, u * PAGE:(u + 1) * PAGE] = _dot_nt(qb_ref[hr, :], head_rows(ck_refs[u], h))
    s = s_ref[...] + base_ref[...]
    shift = rslope_ref[...] * ((p * n_grp - n_pages) * PAGE).astype(F32)
    m_prev = m_ref[...]
    m_new = jnp.maximum(m_prev, jnp.max(s, axis=1, keepdims=True) + shift)
    pm = jnp.exp2(s - (m_new - shift)[:, :1])
    alpha = jnp.exp2(m_prev - m_new)
    l_ref[...] = alpha * l_ref[...] + jnp.sum(pm, axis=1, keepdims=True)
    m_ref[...] = m_new
    pb = pm.astype(BF16)
    for h in range(DA_HEADS):
        hr = slice(h * rows_per_head, (h + 1) * rows_per_head)
        acc = alpha[hr] * acc_ref[hr, :]
        for u in range(n_grp):
            acc = acc + _dot(pb[hr, u * PAGE:(u + 1) * PAGE], head_rows(cv_refs[u], h))
        acc_ref[hr, :] = acc

    @pl.when(p == n_pages // n_grp - 1)
    def _finish():
        s = _dot_nt(qb_ref[...], _pad_rows(kn_ref[0], LANES).astype(BF16)) + basen_ref[...]
        _softmax_step([(s, _pad_rows(vn_ref[0], LANES).astype(BF16))], m_ref, l_ref, acc_ref)
        on = acc_ref[...] / l_ref[...]
        lam = _lambda(lamv_ref, lam_init)
        for h in range(DA_HEADS):
            o = (on[h * rows_per_head:h * rows_per_head + dec_len]
                 - lam * on[h * rows_per_head + dec_len:(h + 1) * rows_per_head])
            o_ref[:, h * DA_VD:(h + 1) * DA_VD] = _rms(o, subln_ref[...], 1e-5) * (1.0 - lam_init)


def _attn_sample(q, kn, vn, ck, cv, page_table, lamv, subln, n_seq, dec_len, samp_row0, lam_init):
    n_pages = page_table.shape[1]
    n_grp = math.gcd(n_pages, PAGES_PER_STEP)
    rows = DA_HEADS * 2 * dec_len
    assert rows == LANES and DA_HEADS * dec_len <= LANES
    page_rows = PAGE * DA_HEADS
    slopes = 2.0 ** (-8.0 * (jnp.arange(rows) // (2 * dec_len) + 1) / DA_HEADS) * LOG2E
    rhead = jnp.arange(rows) // (2 * dec_len)
    rq = jnp.arange(rows) % dec_len
    base = slopes[:, None] * jnp.arange(n_grp * PAGE)[None, :].astype(F32)
    cn = jnp.arange(LANES)
    valid = (cn[None, :] < DA_HEADS * dec_len) & (rhead[:, None] == (cn % DA_HEADS)[None, :]) \
        & ((cn // DA_HEADS)[None, :] <= rq[:, None])
    basen = jnp.where(valid, slopes[:, None] * (cn // DA_HEADS)[None, :].astype(F32), NEG_INF)
    rslope = jnp.broadcast_to(slopes[:, None], (rows, LANES)).astype(F32)
    blk0 = samp_row0 // dec_len
    body = functools.partial(_attn_sample_body, dec_len=dec_len, n_pages=n_pages, n_grp=n_grp, lam_init=lam_init)
    width = DA_HEADS * DA_VD

    def page(u):
        return pl.BlockSpec((1, page_rows, DA_VD), lambda i, p, pt: (pt[i, p * n_grp + u], 0, 0))

    const = lambda a: pl.BlockSpec(a.shape, lambda i, p, pt: (0,) * a.ndim)
    grid_spec = pltpu.PrefetchScalarGridSpec(
        num_scalar_prefetch=1,
        grid=(n_seq, n_pages // n_grp),
        in_specs=[pl.BlockSpec((dec_len, width), lambda i, p, pt: (blk0 + i, 0)),
                  pl.BlockSpec((1,) + kn.shape[1:], lambda i, p, pt: (i, 0, 0)),
                  pl.BlockSpec((1,) + vn.shape[1:], lambda i, p, pt: (i, 0, 0))]
                 + [page(u) for u in range(n_grp)] * 2
                 + [const(base), const(basen), const(rslope), const(lamv), const(subln)],
        out_specs=pl.BlockSpec((dec_len, width), lambda i, p, pt: (i, 0)),
        scratch_shapes=[pltpu.VMEM((rows, DA_VD), BF16), pltpu.VMEM((rows, LANES), F32),
                        pltpu.VMEM((rows, LANES), F32), pltpu.VMEM((rows, DA_VD), F32),
                        pltpu.VMEM((rows, n_grp * PAGE), F32)])
    return pl.pallas_call(
        body,
        out_shape=jax.ShapeDtypeStruct((n_seq * dec_len, width), F32),
        grid_spec=grid_spec,
        compiler_params=_cparams(2),
        name="attn_sample",
    )(page_table, q, kn, vn, *([ck] * n_grp), *([cv] * n_grp), base.astype(F32), basen.astype(F32), rslope,
      lamv, subln)


def kernel(x_prompt, x_sample, state_gla, state_ssd, state_conv, cache_k, cache_v, page_table, meta_tokens, norm_ffn1, w_ffn1_gu, w_ffn1_down, norm_mix, norm_ffn2, w_ffn2_gu, w_ffn2_down, w_in_ab, gla_gk_w2, gla_gk_b, gla_norm, ssd_conv_w, ssd_conv_b, ssd_dt_bias, ssd_a_log, ssd_d, ssd_norm, w_out_ab, w_qkv_c, lam_q1, lam_k1, lam_q2, lam_k2, subln_c, w_out_c, norm_final):
    n_b, seq, d = x_prompt.shape
    n_s, dec, _ = x_sample.shape
    n_meta = meta_tokens.shape[0]
    depth = norm_ffn1.shape[0]
    assert seq % CHUNK == 0 and n_meta % SUBLANES == 0 and dec % SUBLANES == 0
    main_rows, meta_rows, samp_rows = n_b * seq, n_b * n_meta, n_s * dec
    meta_row0, samp_row0 = main_rows, main_rows + meta_rows

    x = jnp.concatenate([x_prompt.reshape(main_rows, d),
                         jnp.tile(meta_tokens.astype(F32), (n_b, 1)),
                         x_sample.reshape(samp_rows, d)], axis=0)

    row = lambda a: a.reshape(1, -1).astype(F32)
    gla_p, gla_s, ssd_p, ssd_s, conv_p, conv_s = [], [], [], [], [], []
    k_p, v_p, k_s, v_s = [], [], [], []
    conv_dim = ssd_conv_w.shape[2]
    gqk, gv = GLA_HEADS * GLA_DK, GLA_HEADS * GLA_DV
    bounds = [0]
    for w in (gqk, gqk, gv, GLA_RANK, gv, SSD_INNER, conv_dim, SSD_HEADS):
        bounds.append(bounds[-1] + w)
    expand = (jnp.arange(SSD_INNER)[None, :] // SSD_HEADDIM == jnp.arange(SSD_HEADS)[:, None]).astype(BF16)

    for l in range(depth):
        x = _ffn(x, row(norm_ffn1[l]), *_ffn_weights(w_ffn1_gu[l], w_ffn1_down[l]), row(norm_final), False)
        if l % 2 == 0:
            e = l // 2
            ws = [w_in_ab[e][:, bounds[i]:bounds[i + 1]].astype(BF16) for i in range(8)]
            proj = _inproj_ab(x, row(norm_mix[l]), ws, gla_gk_w2[e].astype(BF16), row(gla_gk_b[e]),
                              row(ssd_dt_bias[e]))
            params = (row(gla_norm[e]), ssd_conv_w[e].astype(F32), row(ssd_conv_b[e]), row(ssd_a_log[e]),
                      row(jnp.repeat(ssd_d[e], SSD_HEADDIM)), row(ssd_norm[e]), expand)
            ssd_shape = (SSD_INNER, SSD_STATE)
            zg = jnp.zeros((n_b, GLA_HEADS, GLA_DK, GLA_DV), F32)
            zs = jnp.zeros((n_b,) + ssd_shape, F32)
            zc = jnp.zeros((n_b, SSD_CONV - 1, conv_dim), F32)
            mix_meta, sg, ss, cv = _scan(proj, meta_row0, n_b, n_meta, n_meta, zg, zs, zc, params)
            mix_main, sg, ss, cv = _scan(proj, 0, n_b, seq, CHUNK, sg, ss, cv, params)
            gla_p.append(sg)
            ssd_p.append(ss.reshape(n_b, SSD_HEADS, SSD_HEADDIM, SSD_STATE))
            conv_p.append(cv)
            mix_samp, sg, ss, cv = _scan(proj, samp_row0, n_s, dec, dec, state_gla[e].astype(F32),
                                         state_ssd[e].astype(F32).reshape((n_s,) + ssd_shape),
                                         state_conv[e].astype(F32), params)
            gla_s.append(sg)
            ssd_s.append(ss.reshape(n_s, SSD_HEADS, SSD_HEADDIM, SSD_STATE))
            conv_s.append(cv)
            w_out = w_out_ab[e].astype(BF16)
            for mix, row0 in ((mix_main, 0), (mix_meta, meta_row0), (mix_samp, samp_row0)):
                x = _outproj(x, mix, w_out, row0)
        else:
            o = l // 2
            lam_init = 0.8 - 0.6 * math.exp(-0.3 * l)
            width = DA_HEADS * DA_VD
            wq, wk, wv = (w_qkv_c[o][:, i * width:(i + 1) * width].astype(BF16) for i in range(3))
            q, k, v, kb = _qkv(x, row(norm_mix[l]), wq, wk, wv)
            slopes = jnp.exp2(-8.0 * jnp.arange(1, DA_HEADS + 1, dtype=F32) / DA_HEADS) * LOG2E
            lamv = jnp.stack([lam_q1[o], lam_k1[o], lam_q2[o], lam_k2[o]]).astype(F32)
            subln = row(subln_c[o])
            o_main = _attn_prompt(q, kb, v, lamv, subln, n_b, seq, n_meta, meta_row0, lam_init)
            o_meta = _attn_meta(q, k, v, slopes, lamv, subln, n_b, n_meta, meta_row0, lam_init)
            kn = k[samp_row0:].reshape(n_s, dec * DA_HEADS, DA_VD)
            vn = v[samp_row0:].reshape(n_s, dec * DA_HEADS, DA_VD)
            ck = cache_k[o].reshape(cache_k.shape[1], PAGE * DA_HEADS, DA_VD)
            cv_ = cache_v[o].reshape(cache_v.shape[1], PAGE * DA_HEADS, DA_VD)
            o_samp = _attn_sample(q, kn, vn, ck, cv_, page_table, lamv, subln, n_s, dec, samp_row0, lam_init)
            w_out = w_out_c[o].astype(BF16)
            for att, row0 in ((o_main, 0), (o_meta, meta_row0), (o_samp, samp_row0)):
                x = _outproj(x, att, w_out, row0)
            rows5 = lambda a, n, t: a.reshape(n, t, DA_HEADS, DA_VD)
            k_p.append(jnp.concatenate([rows5(k[meta_row0:samp_row0], n_b, n_meta), rows5(k[:main_rows], n_b, seq)], axis=1))
            v_p.append(jnp.concatenate([rows5(v[meta_row0:samp_row0], n_b, n_meta), rows5(v[:main_rows], n_b, seq)], axis=1))
            k_s.append(rows5(k[samp_row0:], n_s, dec))
            v_s.append(rows5(v[samp_row0:], n_s, dec))
        ffn2 = (row(norm_ffn2[l]), *_ffn_weights(w_ffn2_gu[l], w_ffn2_down[l]), row(norm_final))
        if l < depth - 1:
            x = _ffn(x, *ffn2, False)

    y_main = _ffn(x, *ffn2, True, 0, main_rows, 1024)
    y_rest = _ffn(x, *ffn2, True, main_rows, meta_rows + samp_rows, 1024)
    y_prompt = y_main.reshape(n_b, seq, d)
    y_sample = y_rest[meta_rows:].reshape(n_s, dec, d)
    return (y_prompt, y_sample, jnp.stack(gla_p), jnp.stack(gla_s), jnp.stack(ssd_p), jnp.stack(ssd_s),
            jnp.stack(conv_p), jnp.stack(conv_s), jnp.stack(k_p), jnp.stack(v_p), jnp.stack(k_s), jnp.stack(v_s))
```

```python
import functools
import math

import jax
import jax.numpy as jnp
from jax import lax
from jax.experimental import pallas as pl
from jax.experimental.pallas import tpu as pltpu

F32 = jnp.float32
BF16 = jnp.bfloat16

CHUNK = 64
GLA_HEADS, GLA_DK, GLA_DV, GLA_RANK = 4, 128, 256, 16
GLA_GATE_NORM = 16.0
SSD_HEADS, SSD_HEADDIM, SSD_STATE, SSD_GROUPS, SSD_CONV = 16, 64, 64, 2, 4
SSD_INNER = SSD_HEADS * SSD_HEADDIM
DA_HEADS, DA_DH = 8, 64
DA_VD = 2 * DA_DH
PAGE = 128

LANES = 128
SUBLANES = 8
VMEM_LIMIT = 56 * 1024 * 1024

NEG_INF = float("-inf")


def _cparams(n_axes):
    return pltpu.CompilerParams(dimension_semantics=("arbitrary",) * n_axes,
                                vmem_limit_bytes=VMEM_LIMIT)


def _row_tile(n, target):
    best = SUBLANES
    for t in range(SUBLANES, target + 1, SUBLANES):
        if n % t == 0:
            best = t
    return best


def _resident(shape):
    nd = len(shape)
    return pl.BlockSpec(shape, lambda *_: (0,) * nd, pipeline_mode=pl.Buffered(1))


def _rms(x, w, eps):
    ms = jnp.mean(x * x, axis=-1, keepdims=True)
    return x * lax.rsqrt(ms + eps) * w


def _silu(x):
    return x * jax.nn.sigmoid(x)


def _softplus(x):
    return jnp.maximum(x, 0.0) + jnp.log1p(jnp.exp(-jnp.abs(x)))


def _dot(a, b):
    return jnp.dot(a, b, preferred_element_type=F32)


def _dot_nt(a, b):
    return lax.dot_general(a, b, (((1,), (1,)), ((), ())), preferred_element_type=F32)


def _dot_tn(a, b):
    return lax.dot_general(a, b, (((0,), (0,)), ((), ())), preferred_element_type=F32)


def _split3(a):
    hi = a.astype(BF16)
    r1 = a - hi.astype(F32)
    mid = r1.astype(BF16)
    lo = (r1 - mid.astype(F32)).astype(BF16)
    return hi, mid, lo


def _dot01_left(m01, a):
    hi, mid, lo = _split3(a)
    return _dot(m01, hi) + _dot(m01, mid) + _dot(m01, lo)


def _dot01_right(a, m01):
    hi, mid, lo = _split3(a)
    return _dot(hi, m01) + _dot(mid, m01) + _dot(lo, m01)


FFN_CHUNK = 256


def _ffn_body(x_ref, nw_ref, wgu_ref, wd_ref, fw_ref, o_ref, *, final_norm):
    x = x_ref[...]
    xn = _rms(x, nw_ref[...], 1e-6).astype(BF16)
    ff = wd_ref.shape[0]
    acc = None
    for c0 in range(0, ff, FFN_CHUNK):
        g = _dot(xn, wgu_ref[:, c0:c0 + FFN_CHUNK])
        u = _dot(xn, wgu_ref[:, ff + c0:ff + c0 + FFN_CHUNK])
        down = _dot((_silu(g) * u).astype(BF16), wd_ref[c0:c0 + FFN_CHUNK, :])
        acc = down if acc is None else acc + down
    y = x + 0.5 * acc
    if final_norm:
        y = _rms(y, fw_ref[...], 1e-6)
    o_ref[...] = y


def _ffn(x, nw, wgu, wd, fw, final_norm, row0=0, n_rows=None, tile=896):
    n, d = x.shape
    n_rows = n if n_rows is None else n_rows
    assert wd.shape[0] % FFN_CHUNK == 0 and wgu.shape[1] == 2 * wd.shape[0]
    tm = _row_tile(math.gcd(n_rows, row0) if row0 else n_rows, tile)
    blk0 = row0 // tm
    body = functools.partial(_ffn_body, final_norm=final_norm)
    return pl.pallas_call(
        body,
        out_shape=jax.ShapeDtypeStruct((n_rows, d), F32),
        grid=(n_rows // tm,),
        in_specs=[pl.BlockSpec((tm, d), lambda i: (blk0 + i, 0)),
                  _resident(nw.shape), _resident(wgu.shape), _resident(wd.shape), _resident(fw.shape)],
        out_specs=pl.BlockSpec((tm, d), lambda i: (i, 0)),
        compiler_params=_cparams(1),
        name="ffn",
    )(x, nw, wgu, wd, fw)


def _ffn_weights(w_gu, w_down):
    return w_gu.astype(BF16), w_down.astype(BF16)


def _inproj_ab_body(x_ref, nw_ref, wq_ref, wk_ref, wv_ref, wgk_ref, wg_ref, wz_ref, wx_ref, wdt_ref,
                    gk2_ref, gkb_ref, dtb_ref,
                    q_ref, k_ref, v_ref, lf_ref, g_ref, z_ref, xbc_ref, dt_ref):
    xn = _rms(x_ref[...], nw_ref[...], 1e-6).astype(BF16)
    q_ref[...] = _dot(xn, wq_ref[...]) * (GLA_DK ** -0.5)
    k_ref[...] = _dot(xn, wk_ref[...])
    v_ref[...] = _dot(xn, wv_ref[...])
    gk_lo = _dot(xn, wgk_ref[...]).astype(BF16)
    gk = _dot(gk_lo, gk2_ref[...]) + gkb_ref[...]
    lf_ref[...] = -_softplus(-gk) / GLA_GATE_NORM
    g_ref[...] = _dot(xn, wg_ref[...])
    z_ref[...] = _dot(xn, wz_ref[...])
    xbc_ref[...] = _dot(xn, wx_ref[...])
    dt_ref[...] = _softplus(_dot(xn, wdt_ref[...]) + dtb_ref[...])


def _inproj_ab(x, nw, ws, gk2, gkb, dtb):
    n, d = x.shape
    tm = _row_tile(n, 448)
    widths = [w.shape[1] for w in ws]
    out_w = [widths[0], widths[1], widths[2], gk2.shape[1], widths[4], widths[5], widths[6], widths[7]]
    return pl.pallas_call(
        _inproj_ab_body,
        out_shape=[jax.ShapeDtypeStruct((n, w), F32) for w in out_w],
        grid=(n // tm,),
        in_specs=[pl.BlockSpec((tm, d), lambda i: (i, 0)), _resident(nw.shape)]
                 + [_resident(w.shape) for w in ws]
                 + [_resident(gk2.shape), _resident(gkb.shape), _resident(dtb.shape)],
        out_specs=[pl.BlockSpec((tm, w), lambda i: (i, 0)) for w in out_w],
        compiler_params=_cparams(1),
        name="inproj_ab",
    )(x, nw, *ws, gk2, gkb, dtb)


LOG2E = 1.4426950408889634


def _qkv_body(x_ref, nw_ref, wq_ref, wk_ref, wv_ref, q_ref, k_ref, v_ref, kb_ref):
    xn = _rms(x_ref[...], nw_ref[...], 1e-6).astype(BF16)
    q_ref[...] = (_dot(xn, wq_ref[...]) * (DA_DH ** -0.5 * LOG2E)).astype(BF16)
    k = _dot(xn, wk_ref[...])
    k_ref[...] = k
    kb_ref[...] = k.astype(BF16)
    v_ref[...] = _dot(xn, wv_ref[...])


def _qkv(x, nw, wq, wk, wv):
    n, d = x.shape
    tm = _row_tile(n, 896)
    w = wq.shape[1]
    return pl.pallas_call(
        _qkv_body,
        out_shape=[jax.ShapeDtypeStruct((n, w), BF16), jax.ShapeDtypeStruct((n, w), F32),
                   jax.ShapeDtypeStruct((n, w), F32), jax.ShapeDtypeStruct((n, w), BF16)],
        grid=(n // tm,),
        in_specs=[pl.BlockSpec((tm, d), lambda i: (i, 0)), _resident(nw.shape),
                  _resident(wq.shape), _resident(wk.shape), _resident(wv.shape)],
        out_specs=[pl.BlockSpec((tm, w), lambda i: (i, 0))] * 4,
        compiler_params=_cparams(1),
        name="qkv",
    )(x, nw, wq, wk, wv)


def _outproj_body(x_ref, m_ref, w_ref, o_ref):
    o_ref[...] = x_ref[...] + _dot(m_ref[...].astype(BF16), w_ref[...])


def _outproj(x, mix, w, row0):
    n, d = x.shape
    ns, km = mix.shape
    tm = _row_tile(math.gcd(ns, row0) if row0 else ns, 1024)
    blk0 = row0 // tm
    return pl.pallas_call(
        _outproj_body,
        out_shape=jax.ShapeDtypeStruct((n, d), F32),
        grid=(ns // tm,),
        in_specs=[pl.BlockSpec((tm, d), lambda i: (blk0 + i, 0)), pl.BlockSpec((tm, km), lambda i: (i, 0)),
                  _resident(w.shape)],
        out_specs=pl.BlockSpec((tm, d), lambda i: (blk0 + i, 0)),
        input_output_aliases={0: 0},
        compiler_params=_cparams(1),
        name="outproj",
    )(x, mix, w)


HIST = SUBLANES


SEQS_PER_STEP = 2


def _scan_chunk(q_ref, k_ref, v_ref, lf_ref, g_ref, z_ref, xbc_ref, dt_ref,
                gnorm_ref, cw_ref, cb_ref, alog_ref, dskip_ref, snorm_ref, expand_ref,
                mix_ref, st_ref, ss_ref, xpad_ref, y_ref, *, c_len):
    kc = SSD_CONV - 1

    row = lax.broadcasted_iota(jnp.int32, (c_len, c_len), 0)
    col = lax.broadcasted_iota(jnp.int32, (c_len, c_len), 1)
    tril = row >= col
    ltri = jnp.where(tril, 1.0, 0.0).astype(BF16)

    xpad_ref[HIST:HIST + c_len, :] = xbc_ref[...]
    acc = cb_ref[...] + xpad_ref[HIST - kc:HIST - kc + c_len, :] * cw_ref[0:1, :]
    for t in range(1, SSD_CONV):
        acc = acc + xpad_ref[HIST - kc + t:HIST - kc + t + c_len, :] * cw_ref[t:t + 1, :]
    tail = xpad_ref[HIST + c_len - kc:HIST + c_len, :]
    xpad_ref[HIST - kc:HIST, :] = tail
    xc = _silu(acc)
    xs = xc[:, :SSD_INNER]
    gs = SSD_GROUPS * SSD_STATE
    bmat = xc[:, SSD_INNER:SSD_INNER + gs]
    cmat = xc[:, SSD_INNER + gs:SSD_INNER + 2 * gs]

    expand = expand_ref[...]
    dt = dt_ref[...]
    da = dt * (-jnp.exp(alog_ref[...]))
    cs = _dot01_left(ltri, da)
    cs_t = cs.T
    cs_x = _dot01_right(cs, expand)
    dt_x = _dot01_right(dt, expand)
    cs_last_x = cs_x[c_len - 1:c_len, :]
    xdt = xs * dt_x
    e_cs = jnp.exp(cs_x)
    xw = xdt * jnp.exp(cs_last_x - cs_x)
    g_last = jnp.exp(cs_last_x)
    hpg = SSD_HEADS // SSD_GROUPS
    gw = hpg * SSD_HEADDIM
    for gi in range(SSD_GROUPS):
        bg = bmat[:, gi * SSD_STATE:(gi + 1) * SSD_STATE].astype(BF16)
        cg = cmat[:, gi * SSD_STATE:(gi + 1) * SSD_STATE].astype(BF16)
        gmat = _dot_nt(cg, bg)
        s_g = ss_ref[gi * gw:(gi + 1) * gw, :]
        y_inter = _dot_nt(cg, s_g.astype(BF16)) * e_cs[:, gi * gw:(gi + 1) * gw]
        ds_g = _dot_tn(xw[:, gi * gw:(gi + 1) * gw].astype(BF16), bg)
        for hh in range(hpg):
            h = gi * hpg + hh
            lo, hi = h * SSD_HEADDIM, (h + 1) * SSD_HEADDIM
            seg = jnp.broadcast_to(cs[:, h:h + 1], (c_len, c_len)) - cs_t[h:h + 1, :]
            decay = jnp.exp(jnp.where(tril, seg, NEG_INF))
            sc = (gmat * decay).astype(BF16)
            y_intra = _dot(sc, xdt[:, lo:hi].astype(BF16))
            y_ref[:, lo:hi] = y_intra + y_inter[:, hh * SSD_HEADDIM:(hh + 1) * SSD_HEADDIM]
            ss_ref[lo:hi, :] = (g_last[:, lo:hi] * s_g[hh * SSD_HEADDIM:(hh + 1) * SSD_HEADDIM, :]
                                + ds_g[hh * SSD_HEADDIM:(hh + 1) * SSD_HEADDIM, :])
    y = (y_ref[...] + xs * dskip_ref[...]) * _silu(z_ref[...])
    half = SSD_INNER // SSD_GROUPS
    for gi in range(SSD_GROUPS):
        mix_ref[:, GLA_HEADS * GLA_DV + gi * half:GLA_HEADS * GLA_DV + (gi + 1) * half] = _rms(
            y[:, gi * half:(gi + 1) * half], snorm_ref[:, gi * half:(gi + 1) * half], 1e-6).astype(mix_ref.dtype)

    b = _dot01_left(ltri, lf_ref[...])
    b_last = b[c_len - 1:c_len, :]
    q_dec = q_ref[...] * jnp.exp(b)
    k = k_ref[...]
    k_inv = k * jnp.exp(-b)
    k_end = k * jnp.exp(b_last - b)
    g_row = jnp.exp(b_last)
    for h in range(GLA_HEADS):
        ks, ke = h * GLA_DK, (h + 1) * GLA_DK
        vs, ve = h * GLA_DV, (h + 1) * GLA_DV
        qd = q_dec[:, ks:ke].astype(BF16)
        att = jnp.where(tril, _dot_nt(qd, k_inv[:, ks:ke].astype(BF16)), 0.0)
        v_h = v_ref[:, vs:ve].astype(BF16)
        st = st_ref[h]
        o = _dot(att.astype(BF16), v_h) + _dot_nt(qd, st.astype(BF16))
        st_ref[h] = g_row[:, ks:ke] * st + _dot_tn(v_h, k_end[:, ks:ke].astype(BF16))
        mix_ref[:, vs:ve] = (_rms(o, gnorm_ref[...], 1e-6) * _silu(g_ref[:, vs:ve])).astype(mix_ref.dtype)


N_PROJ = 8
N_SCAN_PARAMS = 7


def _scan_body(*refs, c_len, n_steps, sps):
    proj = refs[:N_PROJ * sps]
    gla0_ref, ssd0_ref, conv0_ref = refs[N_PROJ * sps:N_PROJ * sps + 3]
    params = refs[N_PROJ * sps + 3:N_PROJ * sps + 3 + N_SCAN_PARAMS]
    mix_ref, gla_out_ref, ssd_out_ref, conv_out_ref, st_ref, ss_ref, xpad_ref, y_ref = \
        refs[N_PROJ * sps + 3 + N_SCAN_PARAMS:]
    j = pl.program_id(1)
    kc = SSD_CONV - 1
    conv_dim = xpad_ref.shape[2]

    @pl.when(j == 0)
    def _load_state():
        for s in range(sps):
            for h in range(GLA_HEADS):
                st_ref[s * GLA_HEADS + h] = gla0_ref[s, h].T
            ss_ref[s] = ssd0_ref[s]
            xpad_ref[s, 0:HIST, :] = jnp.zeros((HIST, conv_dim), F32)
            xpad_ref[s, HIST - kc:HIST, :] = conv0_ref[s]

    for s in range(sps):
        _scan_chunk(*proj[N_PROJ * s:N_PROJ * (s + 1)], *params, mix_ref.at[s],
                    st_ref.at[pl.ds(s * GLA_HEADS, GLA_HEADS)], ss_ref.at[s], xpad_ref.at[s], y_ref.at[s],
                    c_len=c_len)

    @pl.when(j == n_steps - 1)
    def _store_state():
        for s in range(sps):
            for h in range(GLA_HEADS):
                gla_out_ref[s, h] = st_ref[s * GLA_HEADS + h].T
            ssd_out_ref[s] = ss_ref[s]
            conv_out_ref[s] = xpad_ref[s, HIST - kc:HIST, :]


def _scan(proj, row0, n_seq, seq_len, c_len, gla0, ssd0, conv0, params):
    assert len(proj) == N_PROJ and len(params) == N_SCAN_PARAMS
    n_steps = seq_len // c_len
    blk0 = row0 // c_len
    conv_dim = proj[6].shape[1]
    mix_w = GLA_HEADS * GLA_DV + SSD_INNER
    sps = math.gcd(n_seq, SEQS_PER_STEP)

    def rows(w, s):
        return pl.BlockSpec((c_len, w), lambda b, j: (blk0 + (b * sps + s) * n_steps + j, 0))

    def per_seq(shape):
        nd = len(shape)
        return pl.BlockSpec((sps,) + tuple(shape[1:]), lambda b, j: (b,) + (0,) * (nd - 1))

    body = functools.partial(_scan_body, c_len=c_len, n_steps=n_steps, sps=sps)
    mix, gla, ssd, conv = pl.pallas_call(
        body,
        out_shape=[jax.ShapeDtypeStruct((n_seq, seq_len, mix_w), BF16 if c_len % (2 * SUBLANES) == 0 else F32),
                   jax.ShapeDtypeStruct(gla0.shape, F32),
                   jax.ShapeDtypeStruct(ssd0.shape, F32),
                   jax.ShapeDtypeStruct(conv0.shape, F32)],
        grid=(n_seq // sps, n_steps),
        in_specs=[rows(a.shape[1], s) for s in range(sps) for a in proj]
                 + [per_seq(gla0.shape), per_seq(ssd0.shape), per_seq(conv0.shape)]
                 + [_resident(p.shape) for p in params],
        out_specs=[pl.BlockSpec((sps, c_len, mix_w), lambda b, j: (b, j, 0)),
                   per_seq(gla0.shape), per_seq(ssd0.shape), per_seq(conv0.shape)],
        scratch_shapes=[pltpu.VMEM((sps * GLA_HEADS, GLA_DV, GLA_DK), F32),
                        pltpu.VMEM((sps, SSD_INNER, SSD_STATE), F32),
                        pltpu.VMEM((sps, HIST + c_len, conv_dim), F32),
                        pltpu.VMEM((sps, c_len, SSD_INNER), F32)],
        compiler_params=_cparams(2),
        name=f"scan_c{c_len}",
    )(*(list(proj) * sps), gla0, ssd0, conv0, *params)
    return mix.reshape(n_seq * seq_len, mix_w), gla, ssd, conv


def _softmax_step(blocks, m_ref, l_ref, acc_ref, shift=None):
    m_prev = m_ref[...]
    s_max = None
    for s, _ in blocks:
        bm = jnp.max(s, axis=1, keepdims=True)
        s_max = bm if s_max is None else jnp.maximum(s_max, bm)
    if shift is None:
        m_new = jnp.maximum(m_prev, s_max)
        off = m_new
    else:
        m_new = jnp.maximum(m_prev, s_max + shift)
        off = m_new - shift
    alpha = jnp.exp2(m_prev - m_new)
    l = alpha * l_ref[...]
    acc = alpha * acc_ref[...]
    for s, v in blocks:
        p = jnp.exp2(s - off[:, :1])
        l = l + jnp.sum(p, axis=1, keepdims=True)
        acc = acc + _dot(p.astype(BF16), v)
    l_ref[...] = l
    acc_ref[...] = acc
    m_ref[...] = m_new


def _softmax_step_t(blocks, m_ref, acc_ref, hh):
    m_prev = m_ref[hh]
    m_new = m_prev
    for s, _ in blocks:
        m_new = jnp.maximum(m_new, jnp.max(s, axis=0, keepdims=True))
    acc = jnp.exp2(m_prev - m_new) * acc_ref[hh]
    for s, vt1 in blocks:
        acc = acc + _dot(vt1, jnp.exp2(s - m_new).astype(BF16))
    acc_ref[hh] = acc
    m_ref[hh] = m_new


def _lambda(lamv_ref, lam_init):
    lv = lamv_ref[...]
    s1 = jnp.sum(lv[0:1, :] * lv[1:2, :], axis=1, keepdims=True)
    s2 = jnp.sum(lv[2:3, :] * lv[3:4, :], axis=1, keepdims=True)
    return jnp.exp(s1) - jnp.exp(s2) + lam_init


def _pad_rows(x, n):
    return jnp.concatenate([x, jnp.zeros((n - x.shape[0], x.shape[1]), x.dtype)], axis=0)


def _split_maps(qv):
    lane = lax.broadcasted_iota(jnp.int32, qv.shape, 1)
    return jnp.concatenate([jnp.where(lane < DA_DH, qv, 0.0), jnp.where(lane >= DA_DH, qv, 0.0)], axis=0)


ATT_HEADS_PER_STEP = 2
ONES_ROWS = 16
POS_RADIX = 64


def _attn_prompt_body(q_ref, qn_ref, k_ref, v_ref, km_ref, vm_ref, posk_ref, poskm_ref, qbias_ref, dmask_ref,
                      lamv_ref, subln_ref, o_ref, vt_ref, vtm_ref, q2_ref, m_ref, acc_ref, s_ref,
                      s0_ref, *, tq, lam_init):
    qi = pl.program_id(2)
    seq = k_ref.shape[0]
    dv1 = DA_VD + ONES_ROWS
    cur = qi % 2

    def keys(r0, hh):
        return jnp.concatenate([k_ref[pl.ds(r0, tq), hh * DA_VD:(hh + 1) * DA_VD], posk_ref[pl.ds(r0, tq), :]],
                               axis=1)

    def first_block(slot, src_ref, dst_ref):
        for hh in range(ATT_HEADS_PER_STEP):
            q2_ref[slot, hh, :, 0:DA_VD] = _split_maps(src_ref[:, hh * DA_VD:(hh + 1) * DA_VD])
            q2_ref[slot, hh, :, DA_VD:2 * DA_VD] = jnp.broadcast_to(qbias_ref[hh], (2 * tq, LANES))
            dst_ref[hh] = _dot_nt(keys(0, hh), q2_ref[slot, hh])

    @pl.when(qi == 0)
    def _start_of_sequence():
        for hh in range(ATT_HEADS_PER_STEP):
            lanes = slice(hh * DA_VD, (hh + 1) * DA_VD)
            for c in range(seq // tq):
                vt_ref[hh, 0:DA_VD, c * tq:(c + 1) * tq] = v_ref[c * tq:(c + 1) * tq, lanes].T.astype(BF16)
            vt_ref[hh, DA_VD:dv1, :] = jnp.ones((ONES_ROWS, seq), BF16)
            vtm_ref[hh, 0:DA_VD, :] = _pad_rows(vm_ref[:, lanes], LANES).T.astype(BF16)
            vtm_ref[hh, DA_VD:dv1, :] = jnp.ones((ONES_ROWS, LANES), BF16)
        first_block(0, q_ref, s_ref.at[0])

    for hh in range(ATT_HEADS_PER_STEP):
        m_ref[hh] = jnp.full((1, 2 * tq), NEG_INF, F32)
        acc_ref[hh] = jnp.zeros((dv1, 2 * tq), F32)

    def produce(j, slot):
        r0 = pl.multiple_of(j * tq, tq)
        for hh in range(ATT_HEADS_PER_STEP):
            s_ref[slot, hh] = _dot_nt(keys(r0, hh), q2_ref[cur, hh])

    def consume(j, src_ref):
        r0 = pl.multiple_of(j * tq, tq)
        for hh in range(ATT_HEADS_PER_STEP):
            _softmax_step_t([(src_ref[hh], vt_ref[hh, :, pl.ds(r0, tq)])], m_ref, acc_ref, hh)

    @pl.when(qi >= 1)
    def _block_0():
        consume(0, s0_ref)
        produce(1, 1)

    def two_blocks(t, carry):
        j = 2 * t + 1
        consume(j, s_ref.at[1])
        produce(j + 1, 0)
        consume(j + 1, s_ref.at[0])
        produce(j + 2, 1)
        return carry

    lax.fori_loop(0, jnp.maximum(qi - 1, 0) // 2, two_blocks, 0)

    @pl.when(jnp.logical_and(qi >= 2, cur == 0))
    def _odd_block():
        consume(qi - 1, s_ref.at[1])
        produce(qi, 0)

    r0 = pl.multiple_of(qi * tq, tq)
    lam = _lambda(lamv_ref, lam_init)
    first_block(1 - cur, qn_ref, s0_ref)
    for hh in range(ATT_HEADS_PER_STEP):
        lanes = slice(hh * DA_VD, (hh + 1) * DA_VD)
        s_d = s_ref[cur, hh] + dmask_ref[...]
        n_meta = km_ref.shape[0]
        s_m = _dot_nt(jnp.concatenate([km_ref[:, lanes], poskm_ref[...]], axis=1), q2_ref[cur, hh])
        _softmax_step_t([(s_d, vt_ref[hh, :, pl.ds(r0, tq)]), (s_m, vtm_ref[hh, :, 0:n_meta])],
                        m_ref, acc_ref, hh)
        a = acc_ref[hh]
        on = a[0:DA_VD] * (1.0 / a[DA_VD:DA_VD + 1])
        o_t = on[:, :tq] - lam * on[:, tq:]
        o_ref[:, lanes] = (_rms(o_t.T, subln_ref[...], 1e-5) * (1.0 - lam_init)).astype(o_ref.dtype)


def _alibi_tables(seq_len, n_meta, tq):
    assert seq_len + n_meta <= POS_RADIX * 256
    slopes = jnp.exp2(-8.0 * jnp.arange(1, DA_HEADS + 1, dtype=F32) / DA_HEADS) * LOG2E
    hi, mid, lo = (t.astype(F32) for t in _split3(slopes))
    qbias = jnp.zeros((DA_HEADS, 1, LANES), F32)
    for i, t in enumerate((hi * POS_RADIX, mid * POS_RADIX, lo * POS_RADIX, hi, mid, lo)):
        qbias = qbias.at[:, 0, i].set(t)

    def digits(pos, n_rows):
        d = jnp.zeros((n_rows, LANES), F32)
        for i in range(3):
            d = d.at[:pos.shape[0], i].set((pos // POS_RADIX).astype(F32))
            d = d.at[:pos.shape[0], 3 + i].set((pos % POS_RADIX).astype(F32))
        return d.astype(BF16)

    posk = digits(n_meta + jnp.arange(seq_len), seq_len)
    poskm = digits(jnp.arange(n_meta), n_meta)
    key_t = jnp.arange(tq)[:, None]
    qry = jnp.arange(2 * tq)[None, :] % tq
    dmask = jnp.where(key_t <= qry, 0.0, NEG_INF).astype(F32)
    return posk, poskm, qbias.astype(BF16), dmask


def _attn_prompt(q, kb, v, lamv, subln, n_seq, seq_len, n_meta, meta_row0, lam_init):
    tq = _row_tile(seq_len, 512)
    assert tq % LANES == 0
    nq = seq_len // tq
    hps = ATT_HEADS_PER_STEP
    wblk = hps * DA_VD
    meta_blk0 = meta_row0 // n_meta
    posk, poskm, qbias, dmask = _alibi_tables(seq_len, n_meta, tq)
    dv1 = DA_VD + ONES_ROWS
    body = functools.partial(_attn_prompt_body, tq=tq, lam_init=lam_init)
    const = lambda a: pl.BlockSpec(a.shape, lambda b, h, qi: (0,) * a.ndim, pipeline_mode=pl.Buffered(1))
    return pl.pallas_call(
        body,
        out_shape=jax.ShapeDtypeStruct((n_seq * seq_len, DA_HEADS * DA_VD), BF16),
        grid=(n_seq, DA_HEADS // hps, nq),
        in_specs=[pl.BlockSpec((tq, wblk), lambda b, h, qi: (b * nq + qi, h)),
                  pl.BlockSpec((tq, wblk), lambda b, h, qi: (b * nq + jnp.minimum(qi + 1, nq - 1), h)),
                  pl.BlockSpec((seq_len, wblk), lambda b, h, qi: (b, h)),
                  pl.BlockSpec((seq_len, wblk), lambda b, h, qi: (b, h)),
                  pl.BlockSpec((n_meta, wblk), lambda b, h, qi: (meta_blk0 + b, h)),
                  pl.BlockSpec((n_meta, wblk), lambda b, h, qi: (meta_blk0 + b, h)),
                  const(posk), const(poskm),
                  pl.BlockSpec((hps, 1, LANES), lambda b, h, qi: (h, 0, 0)),
                  const(dmask), const(lamv), const(subln)],
        out_specs=pl.BlockSpec((tq, wblk), lambda b, h, qi: (b * nq + qi, h)),
        scratch_shapes=[pltpu.VMEM((hps, dv1, seq_len), BF16), pltpu.VMEM((hps, dv1, LANES), BF16),
                        pltpu.VMEM((2, hps, 2 * tq, 2 * DA_VD), BF16), pltpu.VMEM((hps, 1, 2 * tq), F32),
                        pltpu.VMEM((hps, dv1, 2 * tq), F32), pltpu.VMEM((2, hps, tq, 2 * tq), F32),
                        pltpu.VMEM((hps, tq, 2 * tq), F32)],
        compiler_params=_cparams(3),
        name="attn_prompt",
    )(q, q, kb, v, kb, v, posk, poskm, qbias, dmask, lamv, subln)


def _attn_meta_body(slopes_ref, q_ref, k_ref, v_ref, lamv_ref, subln_ref, o_ref,
                    m_ref, l_ref, acc_ref, *, n_meta, lam_init):
    h = pl.program_id(1)
    slope = slopes_ref[h]
    m_rows = 2 * n_meta
    m_ref[...] = jnp.full(m_ref.shape, NEG_INF, F32)
    l_ref[...] = jnp.zeros(l_ref.shape, F32)
    acc_ref[...] = jnp.zeros(acc_ref.shape, F32)
    qb = _split_maps(q_ref[...])
    r = lax.broadcasted_iota(jnp.int32, (m_rows, LANES), 0)
    r = jnp.where(r >= n_meta, r - n_meta, r)
    c = lax.broadcasted_iota(jnp.int32, (m_rows, LANES), 1)
    bias = jnp.where(c <= r, slope * c.astype(F32), NEG_INF)
    s = _dot_nt(qb, _pad_rows(k_ref[...], LANES).astype(BF16)) + bias
    _softmax_step([(s, _pad_rows(v_ref[...], LANES).astype(BF16))], m_ref, l_ref, acc_ref)
    on = acc_ref[...] / l_ref[...]
    o = on[:n_meta] - _lambda(lamv_ref, lam_init) * on[n_meta:]
    o_ref[...] = _rms(o, subln_ref[...], 1e-5) * (1.0 - lam_init)


def _attn_meta(q, k, v, slopes, lamv, subln, n_seq, n_meta, meta_row0, lam_init):
    blk0 = meta_row0 // n_meta
    spec = pl.BlockSpec((n_meta, DA_VD), lambda b, h: (blk0 + b, h))
    body = functools.partial(_attn_meta_body, n_meta=n_meta, lam_init=lam_init)
    return pl.pallas_call(
        body,
        out_shape=jax.ShapeDtypeStruct((n_seq * n_meta, DA_HEADS * DA_VD), F32),
        grid=(n_seq, DA_HEADS),
        in_specs=[pl.BlockSpec(memory_space=pltpu.SMEM), spec, spec, spec,
                  pl.BlockSpec(lamv.shape, lambda b, h: (0, 0)),
                  pl.BlockSpec(subln.shape, lambda b, h: (0, 0))],
        out_specs=pl.BlockSpec((n_meta, DA_VD), lambda b, h: (b, h)),
        scratch_shapes=[pltpu.VMEM((2 * n_meta, LANES), F32), pltpu.VMEM((2 * n_meta, LANES), F32),
                        pltpu.VMEM((2 * n_meta, DA_VD), F32)],
        compiler_params=_cparams(2),
        name="attn_meta",
    )(slopes, q, k, v, lamv, subln)


PAGES_PER_STEP = 8


def _attn_sample_body(pt_ref, q_ref, kn_ref, vn_ref, *rest, dec_len, n_pages, n_grp, lam_init):
    ck_refs, cv_refs = rest[:n_grp], rest[n_grp:2 * n_grp]
    (base_ref, basen_ref, rslope_ref, lamv_ref, subln_ref, o_ref,
     qb_ref, m_ref, l_ref, acc_ref, s_ref) = rest[2 * n_grp:]
    p = pl.program_id(1)
    rows_per_head = 2 * dec_len

    @pl.when(p == 0)
    def _start():
        lane = lax.broadcasted_iota(jnp.int32, (dec_len, DA_VD), 1)
        for h in range(DA_HEADS):
            qh = q_ref[:, h * DA_VD:(h + 1) * DA_VD]
            qb_ref[h * rows_per_head:h * rows_per_head + dec_len, :] = jnp.where(lane < DA_DH, qh, 0.0)
            qb_ref[h * rows_per_head + dec_len:(h + 1) * rows_per_head, :] = jnp.where(lane >= DA_DH, qh, 0.0)
        m_ref[...] = jnp.full(m_ref.shape, NEG_INF, F32)
        l_ref[...] = jnp.zeros(l_ref.shape, F32)
        acc_ref[...] = jnp.zeros(acc_ref.shape, F32)

    def head_rows(ref, h):
        return ref[0, pl.ds(h, PAGE, stride=DA_HEADS), :].astype(BF16)

    for u in range(n_grp):
        for h in range(DA_HEADS):
            hr = slice(h * rows_per_head, (h + 1) * rows_per_head)
            s_ref[hr, u * PAGE:(u + 1) * PAGE] = _dot_nt(qb_ref[hr, :], head_rows(ck_refs[u], h))
    s = s_ref[...] + base_ref[...]
    shift = rslope_ref[...] * ((p * n_grp - n_pages) * PAGE).astype(F32)
    m_prev = m_ref[...]
    m_new = jnp.maximum(m_prev, jnp.max(s, axis=1, keepdims=True) + shift)
    pm = jnp.exp2(s - (m_new - shift)[:, :1])
    alpha = jnp.exp2(m_prev - m_new)
    l_ref[...] = alpha * l_ref[...] + jnp.sum(pm, axis=1, keepdims=True)
    m_ref[...] = m_new
    pb = pm.astype(BF16)
    for h in range(DA_HEADS):
        hr = slice(h * rows_per_head, (h + 1) * rows_per_head)
        acc = alpha[hr] * acc_ref[hr, :]
        for u in range(n_grp):
            acc = acc + _dot(pb[hr, u * PAGE:(u + 1) * PAGE], head_rows(cv_refs[u], h))
        acc_ref[hr, :] = acc

    @pl.when(p == n_pages // n_grp - 1)
    def _finish():
        s = _dot_nt(qb_ref[...], _pad_rows(kn_ref[0], LANES).astype(BF16)) + basen_ref[...]
        _softmax_step([(s, _pad_rows(vn_ref[0], LANES).astype(BF16))], m_ref, l_ref, acc_ref)
        on = acc_ref[...] / l_ref[...]
        lam = _lambda(lamv_ref, lam_init)
        for h in range(DA_HEADS):
            o = (on[h * rows_per_head:h * rows_per_head + dec_len]
                 - lam * on[h * rows_per_head + dec_len:(h + 1) * rows_per_head])
            o_ref[:, h * DA_VD:(h + 1) * DA_VD] = _rms(o, subln_ref[...], 1e-5) * (1.0 - lam_init)


def _attn_sample(q, kn, vn, ck, cv, page_table, lamv, subln, n_seq, dec_len, samp_row0, lam_init):
    n_pages = page_table.shape[1]
    n_grp = math.gcd(n_pages, PAGES_PER_STEP)
    rows = DA_HEADS * 2 * dec_len
    assert rows == LANES and DA_HEADS * dec_len <= LANES
    page_rows = PAGE * DA_HEADS
    slopes = 2.0 ** (-8.0 * (jnp.arange(rows) // (2 * dec_len) + 1) / DA_HEADS) * LOG2E
    rhead = jnp.arange(rows) // (2 * dec_len)
    rq = jnp.arange(rows) % dec_len
    base = slopes[:, None] * jnp.arange(n_grp * PAGE)[None, :].astype(F32)
    cn = jnp.arange(LANES)
    valid = (cn[None, :] < DA_HEADS * dec_len) & (rhead[:, None] == (cn % DA_HEADS)[None, :]) \
        & ((cn // DA_HEADS)[None, :] <= rq[:, None])
    basen = jnp.where(valid, slopes[:, None] * (cn // DA_HEADS)[None, :].astype(F32), NEG_INF)
    rslope = jnp.broadcast_to(slopes[:, None], (rows, LANES)).astype(F32)
    blk0 = samp_row0 // dec_len
    body = functools.partial(_attn_sample_body, dec_len=dec_len, n_pages=n_pages, n_grp=n_grp, lam_init=lam_init)
    width = DA_HEADS * DA_VD

    def page(u):
        return pl.BlockSpec((1, page_rows, DA_VD), lambda i, p, pt: (pt[i, p * n_grp + u], 0, 0))

    const = lambda a: pl.BlockSpec(a.shape, lambda i, p, pt: (0,) * a.ndim)
    grid_spec = pltpu.PrefetchScalarGridSpec(
        num_scalar_prefetch=1,
        grid=(n_seq, n_pages // n_grp),
        in_specs=[pl.BlockSpec((dec_len, width), lambda i, p, pt: (blk0 + i, 0)),
                  pl.BlockSpec((1,) + kn.shape[1:], lambda i, p, pt: (i, 0, 0)),
                  pl.BlockSpec((1,) + vn.shape[1:], lambda i, p, pt: (i, 0, 0))]
                 + [page(u) for u in range(n_grp)] * 2
                 + [const(base), const(basen), const(rslope), const(lamv), const(subln)],
        out_specs=pl.BlockSpec((dec_len, width), lambda i, p, pt: (i, 0)),
        scratch_shapes=[pltpu.VMEM((rows, DA_VD), BF16), pltpu.VMEM((rows, LANES), F32),
                        pltpu.VMEM((rows, LANES), F32), pltpu.VMEM((rows, DA_VD), F32),
                        pltpu.VMEM((rows, n_grp * PAGE), F32)])
    return pl.pallas_call(
        body,
        out_shape=jax.ShapeDtypeStruct((n_seq * dec_len, width), F32),
        grid_spec=grid_spec,
        compiler_params=_cparams(2),
        name="attn_sample",
    )(page_table, q, kn, vn, *([ck] * n_grp), *([cv] * n_grp), base.astype(F32), basen.astype(F32), rslope,
      lamv, subln)


def _prompt_copies(src_ref, dst_ref, sem, n_seq, seq_len, n_meta, meta_row0):
    copies = []
    for b in range(n_seq):
        copies.append(pltpu.make_async_copy(src_ref.at[pl.ds(meta_row0 + b * n_meta, n_meta)],
                                            dst_ref.at[b, pl.ds(0, n_meta)], sem.at[2 * b]))
        copies.append(pltpu.make_async_copy(src_ref.at[pl.ds(b * seq_len, seq_len)],
                                            dst_ref.at[b, pl.ds(n_meta, seq_len)], sem.at[2 * b + 1]))
    return copies


def _prompt_rows_body(src_ref, dst_ref, sem, **dims):
    copies = _prompt_copies(src_ref, dst_ref, sem, **dims)
    for c in copies:
        c.start()
    for c in copies:
        c.wait()


def _prompt_rows(src, n_seq, seq_len, n_meta, meta_row0):
    body = functools.partial(_prompt_rows_body, n_seq=n_seq, seq_len=seq_len, n_meta=n_meta, meta_row0=meta_row0)
    return pl.pallas_call(
        body,
        out_shape=jax.ShapeDtypeStruct((n_seq, n_meta + seq_len, src.shape[1]), src.dtype),
        in_specs=[pl.BlockSpec(memory_space=pl.ANY)],
        out_specs=pl.BlockSpec(memory_space=pl.ANY),
        scratch_shapes=[pltpu.SemaphoreType.DMA((2 * n_seq,))],
        name="prompt_rows",
    )(src)


def kernel(x_prompt, x_sample, state_gla, state_ssd, state_conv, cache_k, cache_v, page_table, meta_tokens, norm_ffn1, w_ffn1_gu, w_ffn1_down, norm_mix, norm_ffn2, w_ffn2_gu, w_ffn2_down, w_in_ab, gla_gk_w2, gla_gk_b, gla_norm, ssd_conv_w, ssd_conv_b, ssd_dt_bias, ssd_a_log, ssd_d, ssd_norm, w_out_ab, w_qkv_c, lam_q1, lam_k1, lam_q2, lam_k2, subln_c, w_out_c, norm_final):
    n_b, seq, d = x_prompt.shape
    n_s, dec, _ = x_sample.shape
    n_meta = meta_tokens.shape[0]
    depth = norm_ffn1.shape[0]
    assert seq % CHUNK == 0 and n_meta % SUBLANES == 0 and dec % SUBLANES == 0
    main_rows, meta_rows, samp_rows = n_b * seq, n_b * n_meta, n_s * dec
    meta_row0, samp_row0 = main_rows, main_rows + meta_rows

    x = jnp.concatenate([x_prompt.reshape(main_rows, d),
                         jnp.tile(meta_tokens.astype(F32), (n_b, 1)),
                         x_sample.reshape(samp_rows, d)], axis=0)

    row = lambda a: a.reshape(1, -1).astype(F32)
    gla_p, gla_s, ssd_p, ssd_s, conv_p, conv_s = [], [], [], [], [], []
    k_p, v_p, k_s, v_s = [], [], [], []
    conv_dim = ssd_conv_w.shape[2]
    gqk, gv = GLA_HEADS * GLA_DK, GLA_HEADS * GLA_DV
    bounds = [0]
    for w in (gqk, gqk, gv, GLA_RANK, gv, SSD_INNER, conv_dim, SSD_HEADS):
        bounds.append(bounds[-1] + w)
    expand = (jnp.arange(SSD_INNER)[None, :] // SSD_HEADDIM == jnp.arange(SSD_HEADS)[:, None]).astype(BF16)

    for l in range(depth):
        x = _ffn(x, row(norm_ffn1[l]), *_ffn_weights(w_ffn1_gu[l], w_ffn1_down[l]), row(norm_final), False)
        if l % 2 == 0:
            e = l // 2
            ws = [w_in_ab[e][:, bounds[i]:bounds[i + 1]].astype(BF16) for i in range(8)]
            proj = _inproj_ab(x, row(norm_mix[l]), ws, gla_gk_w2[e].astype(BF16), row(gla_gk_b[e]),
                              row(ssd_dt_bias[e]))
            params = (row(gla_norm[e]), ssd_conv_w[e].astype(F32), row(ssd_conv_b[e]), row(ssd_a_log[e]),
                      row(jnp.repeat(ssd_d[e], SSD_HEADDIM)), row(ssd_norm[e]), expand)
            ssd_shape = (SSD_INNER, SSD_STATE)
            zg = jnp.zeros((n_b, GLA_HEADS, GLA_DK, GLA_DV), F32)
            zs = jnp.zeros((n_b,) + ssd_shape, F32)
            zc = jnp.zeros((n_b, SSD_CONV - 1, conv_dim), F32)
            mix_meta, sg, ss, cv = _scan(proj, meta_row0, n_b, n_meta, n_meta, zg, zs, zc, params)
            mix_main, sg, ss, cv = _scan(proj, 0, n_b, seq, CHUNK, sg, ss, cv, params)
            gla_p.append(sg)
            ssd_p.append(ss.reshape(n_b, SSD_HEADS, SSD_HEADDIM, SSD_STATE))
            conv_p.append(cv)
            mix_samp, sg, ss, cv = _scan(proj, samp_row0, n_s, dec, dec, state_gla[e].astype(F32),
                                         state_ssd[e].astype(F32).reshape((n_s,) + ssd_shape),
                                         state_conv[e].astype(F32), params)
            gla_s.append(sg)
            ssd_s.append(ss.reshape(n_s, SSD_HEADS, SSD_HEADDIM, SSD_STATE))
            conv_s.append(cv)
            w_out = w_out_ab[e].astype(BF16)
            for mix, row0 in ((mix_main, 0), (mix_meta, meta_row0), (mix_samp, samp_row0)):
                x = _outproj(x, mix, w_out, row0)
        else:
            o = l // 2
            lam_init = 0.8 - 0.6 * math.exp(-0.3 * l)
            width = DA_HEADS * DA_VD
            wq, wk, wv = (w_qkv_c[o][:, i * width:(i + 1) * width].astype(BF16) for i in range(3))
            q, k, v, kb = _qkv(x, row(norm_mix[l]), wq, wk, wv)
            slopes = jnp.exp2(-8.0 * jnp.arange(1, DA_HEADS + 1, dtype=F32) / DA_HEADS) * LOG2E
            lamv = jnp.stack([lam_q1[o], lam_k1[o], lam_q2[o], lam_k2[o]]).astype(F32)
            subln = row(subln_c[o])
            o_main = _attn_prompt(q, kb, v, lamv, subln, n_b, seq, n_meta, meta_row0, lam_init)
            o_meta = _attn_meta(q, k, v, slopes, lamv, subln, n_b, n_meta, meta_row0, lam_init)
            kn = k[samp_row0:].reshape(n_s, dec * DA_HEADS, DA_VD)
            vn = v[samp_row0:].reshape(n_s, dec * DA_HEADS, DA_VD)
            ck = cache_k[o].reshape(cache_k.shape[1], PAGE * DA_HEADS, DA_VD)
            cv_ = cache_v[o].reshape(cache_v.shape[1], PAGE * DA_HEADS, DA_VD)
            o_samp = _attn_sample(q, kn, vn, ck, cv_, page_table, lamv, subln, n_s, dec, samp_row0, lam_init)
            w_out = w_out_c[o].astype(BF16)
            for att, row0 in ((o_main, 0), (o_meta, meta_row0), (o_samp, samp_row0)):
                x = _outproj(x, att, w_out, row0)
            rows5 = lambda a, n, t: a.reshape(n, t, DA_HEADS, DA_VD)
            k_p.append(rows5(_prompt_rows(k, n_b, seq, n_meta, meta_row0), n_b, n_meta + seq))
            v_p.append(rows5(_prompt_rows(v, n_b, seq, n_meta, meta_row0), n_b, n_meta + seq))
            k_s.append(rows5(k[samp_row0:], n_s, dec))
            v_s.append(rows5(v[samp_row0:], n_s, dec))
        ffn2 = (row(norm_ffn2[l]), *_ffn_weights(w_ffn2_gu[l], w_ffn2_down[l]), row(norm_final))
        if l < depth - 1:
            x = _ffn(x, *ffn2, False)

    y_main = _ffn(x, *ffn2, True, 0, main_rows, 1024)
    y_rest = _ffn(x, *ffn2, True, main_rows, meta_rows + samp_rows, 1024)
    y_prompt = y_main.reshape(n_b, seq, d)
    y_sample = y_rest[meta_rows:].reshape(n_s, dec, d)
    return (y_prompt, y_sample, jnp.stack(gla_p), jnp.stack(gla_s), jnp.stack(ssd_p), jnp.stack(ssd_s),
            jnp.stack(conv_p), jnp.stack(conv_s), jnp.stack(k_p), jnp.stack(v_p), jnp.stack(k_s), jnp.stack(v_s))
```

```python
import functools
import math

import jax
import jax.numpy as jnp
from jax import lax
from jax.experimental import pallas as pl
from jax.experimental.pallas import tpu as pltpu

F32 = jnp.float32
BF16 = jnp.bfloat16

CHUNK = 64
GLA_HEADS, GLA_DK, GLA_DV, GLA_RANK = 4, 128, 256, 16
GLA_GATE_NORM = 16.0
SSD_HEADS, SSD_HEADDIM, SSD_STATE, SSD_GROUPS, SSD_CONV = 16, 64, 64, 2, 4
SSD_INNER = SSD_HEADS * SSD_HEADDIM
DA_HEADS, DA_DH = 8, 64
DA_VD = 2 * DA_DH
PAGE = 128

LANES = 128
SUBLANES = 8
VMEM_LIMIT = 56 * 1024 * 1024

NEG_INF = float("-inf")


def _cparams(n_axes):
    return pltpu.CompilerParams(dimension_semantics=("arbitrary",) * n_axes,
                                vmem_limit_bytes=VMEM_LIMIT)


def _row_tile(n, target):
    best = SUBLANES
    for t in range(SUBLANES, target + 1, SUBLANES):
        if n % t == 0:
            best = t
    return best


def _resident(shape):
    nd = len(shape)
    return pl.BlockSpec(shape, lambda *_: (0,) * nd, pipeline_mode=pl.Buffered(1))


def _rms(x, w, eps):
    ms = jnp.mean(x * x, axis=-1, keepdims=True)
    return x * lax.rsqrt(ms + eps) * w


def _silu(x):
    return x * jax.nn.sigmoid(x)


def _softplus(x):
    return jnp.maximum(x, 0.0) + jnp.log1p(jnp.exp(-jnp.abs(x)))


def _dot(a, b):
    return jnp.dot(a, b, preferred_element_type=F32)


def _dot_nt(a, b):
    return lax.dot_general(a, b, (((1,), (1,)), ((), ())), preferred_element_type=F32)


def _dot_tn(a, b):
    return lax.dot_general(a, b, (((0,), (0,)), ((), ())), preferred_element_type=F32)


def _split3(a):
    hi = a.astype(BF16)
    r1 = a - hi.astype(F32)
    mid = r1.astype(BF16)
    lo = (r1 - mid.astype(F32)).astype(BF16)
    return hi, mid, lo


def _dot01_left(m01, a):
    hi, mid, lo = _split3(a)
    return _dot(m01, hi) + _dot(m01, mid) + _dot(m01, lo)


def _dot01_right(a, m01):
    hi, mid, lo = _split3(a)
    return _dot(hi, m01) + _dot(mid, m01) + _dot(lo, m01)


FFN_CHUNK = 256


def _ffn_body(x_ref, nw_ref, wgu_ref, wd_ref, fw_ref, o_ref, *, final_norm):
    x = x_ref[...]
    xn = _rms(x, nw_ref[...], 1e-6).astype(BF16)
    ff = wd_ref.shape[0]
    acc = None
    for c0 in range(0, ff, FFN_CHUNK):
        g = _dot(xn, wgu_ref[:, c0:c0 + FFN_CHUNK])
        u = _dot(xn, wgu_ref[:, ff + c0:ff + c0 + FFN_CHUNK])
        down = _dot((_silu(g) * u).astype(BF16), wd_ref[c0:c0 + FFN_CHUNK, :])
        acc = down if acc is None else acc + down
    y = x + 0.5 * acc
    if final_norm:
        y = _rms(y, fw_ref[...], 1e-6)
    o_ref[...] = y


def _ffn(x, nw, wgu, wd, fw, final_norm, row0=0, n_rows=None, tile=896):
    n, d = x.shape
    n_rows = n if n_rows is None else n_rows
    assert wd.shape[0] % FFN_CHUNK == 0 and wgu.shape[1] == 2 * wd.shape[0]
    tm = _row_tile(math.gcd(n_rows, row0) if row0 else n_rows, tile)
    blk0 = row0 // tm
    body = functools.partial(_ffn_body, final_norm=final_norm)
    return pl.pallas_call(
        body,
        out_shape=jax.ShapeDtypeStruct((n_rows, d), F32),
        grid=(n_rows // tm,),
        in_specs=[pl.BlockSpec((tm, d), lambda i: (blk0 + i, 0)),
                  _resident(nw.shape), _resident(wgu.shape), _resident(wd.shape), _resident(fw.shape)],
        out_specs=pl.BlockSpec((tm, d), lambda i: (i, 0)),
        compiler_params=_cparams(1),
        name="ffn",
    )(x, nw, wgu, wd, fw)


def _ffn_weights(w_gu, w_down):
    return w_gu.astype(BF16), w_down.astype(BF16)


def _inproj_ab_body(x_ref, nw_ref, wq_ref, wk_ref, wv_ref, wgk_ref, wg_ref, wz_ref, wx_ref, wdt_ref,
                    gk2_ref, gkb_ref, dtb_ref,
                    q_ref, k_ref, v_ref, lf_ref, g_ref, z_ref, xbc_ref, dt_ref):
    xn = _rms(x_ref[...], nw_ref[...], 1e-6).astype(BF16)
    q_ref[...] = _dot(xn, wq_ref[...]) * (GLA_DK ** -0.5)
    k_ref[...] = _dot(xn, wk_ref[...])
    v_ref[...] = _dot(xn, wv_ref[...])
    gk_lo = _dot(xn, wgk_ref[...]).astype(BF16)
    gk = _dot(gk_lo, gk2_ref[...]) + gkb_ref[...]
    lf_ref[...] = -_softplus(-gk) / GLA_GATE_NORM
    g_ref[...] = _dot(xn, wg_ref[...])
    z_ref[...] = _dot(xn, wz_ref[...])
    xbc_ref[...] = _dot(xn, wx_ref[...])
    dt_ref[...] = _softplus(_dot(xn, wdt_ref[...]) + dtb_ref[...])


def _inproj_ab(x, nw, ws, gk2, gkb, dtb):
    n, d = x.shape
    tm = _row_tile(n, 448)
    widths = [w.shape[1] for w in ws]
    out_w = [widths[0], widths[1], widths[2], gk2.shape[1], widths[4], widths[5], widths[6], widths[7]]
    return pl.pallas_call(
        _inproj_ab_body,
        out_shape=[jax.ShapeDtypeStruct((n, w), F32) for w in out_w],
        grid=(n // tm,),
        in_specs=[pl.BlockSpec((tm, d), lambda i: (i, 0)), _resident(nw.shape)]
                 + [_resident(w.shape) for w in ws]
                 + [_resident(gk2.shape), _resident(gkb.shape), _resident(dtb.shape)],
        out_specs=[pl.BlockSpec((tm, w), lambda i: (i, 0)) for w in out_w],
        compiler_params=_cparams(1),
        name="inproj_ab",
    )(x, nw, *ws, gk2, gkb, dtb)


LOG2E = 1.4426950408889634


def _qkv_body(x_ref, nw_ref, wq_ref, wk_ref, wv_ref, q_ref, k_ref, v_ref, kb_ref):
    xn = _rms(x_ref[...], nw_ref[...], 1e-6).astype(BF16)
    q_ref[...] = (_dot(xn, wq_ref[...]) * (DA_DH ** -0.5 * LOG2E)).astype(BF16)
    k = _dot(xn, wk_ref[...])
    k_ref[...] = k
    kb_ref[...] = k.astype(BF16)
    v_ref[...] = _dot(xn, wv_ref[...])


def _qkv(x, nw, wq, wk, wv, row0, n_rows):
    d = x.shape[1]
    tm = _row_tile(math.gcd(n_rows, row0) if row0 else n_rows, 1024)
    blk0 = row0 // tm
    w = wq.shape[1]
    return pl.pallas_call(
        _qkv_body,
        out_shape=[jax.ShapeDtypeStruct((n_rows, w), BF16), jax.ShapeDtypeStruct((n_rows, w), F32),
                   jax.ShapeDtypeStruct((n_rows, w), F32), jax.ShapeDtypeStruct((n_rows, w), BF16)],
        grid=(n_rows // tm,),
        in_specs=[pl.BlockSpec((tm, d), lambda i: (blk0 + i, 0)), _resident(nw.shape),
                  _resident(wq.shape), _resident(wk.shape), _resident(wv.shape)],
        out_specs=[pl.BlockSpec((tm, w), lambda i: (i, 0))] * 4,
        compiler_params=_cparams(1),
        name="qkv",
    )(x, nw, wq, wk, wv)


def _outproj_body(x_ref, m_ref, w_ref, o_ref):
    o_ref[...] = x_ref[...] + _dot(m_ref[...].astype(BF16), w_ref[...])


def _outproj(x, mix, w, row0):
    n, d = x.shape
    ns, km = mix.shape
    tm = _row_tile(math.gcd(ns, row0) if row0 else ns, 1024)
    blk0 = row0 // tm
    return pl.pallas_call(
        _outproj_body,
        out_shape=jax.ShapeDtypeStruct((n, d), F32),
        grid=(ns // tm,),
        in_specs=[pl.BlockSpec((tm, d), lambda i: (blk0 + i, 0)), pl.BlockSpec((tm, km), lambda i: (i, 0)),
                  _resident(w.shape)],
        out_specs=pl.BlockSpec((tm, d), lambda i: (blk0 + i, 0)),
        input_output_aliases={0: 0},
        compiler_params=_cparams(1),
        name="outproj",
    )(x, mix, w)


HIST = SUBLANES


SEQS_PER_STEP = 2


def _scan_chunk(q_ref, k_ref, v_ref, lf_ref, g_ref, z_ref, xbc_ref, dt_ref,
                gnorm_ref, cw_ref, cb_ref, alog_ref, dskip_ref, snorm_ref, expand_ref,
                mix_ref, st_ref, ss_ref, xpad_ref, y_ref, *, c_len):
    kc = SSD_CONV - 1

    row = lax.broadcasted_iota(jnp.int32, (c_len, c_len), 0)
    col = lax.broadcasted_iota(jnp.int32, (c_len, c_len), 1)
    tril = row >= col
    ltri = jnp.where(tril, 1.0, 0.0).astype(BF16)

    xpad_ref[HIST:HIST + c_len, :] = xbc_ref[...]
    acc = cb_ref[...] + xpad_ref[HIST - kc:HIST - kc + c_len, :] * cw_ref[0:1, :]
    for t in range(1, SSD_CONV):
        acc = acc + xpad_ref[HIST - kc + t:HIST - kc + t + c_len, :] * cw_ref[t:t + 1, :]
    tail = xpad_ref[HIST + c_len - kc:HIST + c_len, :]
    xpad_ref[HIST - kc:HIST, :] = tail
    xc = _silu(acc)
    xs = xc[:, :SSD_INNER]
    gs = SSD_GROUPS * SSD_STATE
    bmat = xc[:, SSD_INNER:SSD_INNER + gs]
    cmat = xc[:, SSD_INNER + gs:SSD_INNER + 2 * gs]

    expand = expand_ref[...]
    dt = dt_ref[...]
    da = dt * (-jnp.exp(alog_ref[...]))
    cs = _dot01_left(ltri, da)
    cs_t = cs.T
    cs_x = _dot01_right(cs, expand)
    dt_x = _dot01_right(dt, expand)
    cs_last_x = cs_x[c_len - 1:c_len, :]
    xdt = xs * dt_x
    e_cs = jnp.exp(cs_x)
    xw = xdt * jnp.exp(cs_last_x - cs_x)
    g_last = jnp.exp(cs_last_x)
    hpg = SSD_HEADS // SSD_GROUPS
    gw = hpg * SSD_HEADDIM
    for gi in range(SSD_GROUPS):
        bg = bmat[:, gi * SSD_STATE:(gi + 1) * SSD_STATE].astype(BF16)
        cg = cmat[:, gi * SSD_STATE:(gi + 1) * SSD_STATE].astype(BF16)
        gmat = _dot_nt(cg, bg)
        s_g = ss_ref[gi * gw:(gi + 1) * gw, :]
        y_inter = _dot_nt(cg, s_g.astype(BF16)) * e_cs[:, gi * gw:(gi + 1) * gw]
        ds_g = _dot_tn(xw[:, gi * gw:(gi + 1) * gw].astype(BF16), bg)
        for hh in range(hpg):
            h = gi * hpg + hh
            lo, hi = h * SSD_HEADDIM, (h + 1) * SSD_HEADDIM
            seg = jnp.broadcast_to(cs[:, h:h + 1], (c_len, c_len)) - cs_t[h:h + 1, :]
            decay = jnp.exp(jnp.where(tril, seg, NEG_INF))
            sc = (gmat * decay).astype(BF16)
            y_intra = _dot(sc, xdt[:, lo:hi].astype(BF16))
            y_ref[:, lo:hi] = y_intra + y_inter[:, hh * SSD_HEADDIM:(hh + 1) * SSD_HEADDIM]
            ss_ref[lo:hi, :] = (g_last[:, lo:hi] * s_g[hh * SSD_HEADDIM:(hh + 1) * SSD_HEADDIM, :]
                                + ds_g[hh * SSD_HEADDIM:(hh + 1) * SSD_HEADDIM, :])
    y = (y_ref[...] + xs * dskip_ref[...]) * _silu(z_ref[...])
    half = SSD_INNER // SSD_GROUPS
    for gi in range(SSD_GROUPS):
        mix_ref[:, GLA_HEADS * GLA_DV + gi * half:GLA_HEADS * GLA_DV + (gi + 1) * half] = _rms(
            y[:, gi * half:(gi + 1) * half], snorm_ref[:, gi * half:(gi + 1) * half], 1e-6).astype(mix_ref.dtype)

    b = _dot01_left(ltri, lf_ref[...])
    b_last = b[c_len - 1:c_len, :]
    q_dec = q_ref[...] * jnp.exp(b)
    k = k_ref[...]
    k_inv = k * jnp.exp(-b)
    k_end = k * jnp.exp(b_last - b)
    g_row = jnp.exp(b_last)
    for h in range(GLA_HEADS):
        ks, ke = h * GLA_DK, (h + 1) * GLA_DK
        vs, ve = h * GLA_DV, (h + 1) * GLA_DV
        qd = q_dec[:, ks:ke].astype(BF16)
        att = jnp.where(tril, _dot_nt(qd, k_inv[:, ks:ke].astype(BF16)), 0.0)
        v_h = v_ref[:, vs:ve].astype(BF16)
        st = st_ref[h]
        o = _dot(att.astype(BF16), v_h) + _dot_nt(qd, st.astype(BF16))
        st_ref[h] = g_row[:, ks:ke] * st + _dot_tn(v_h, k_end[:, ks:ke].astype(BF16))
        mix_ref[:, vs:ve] = (_rms(o, gnorm_ref[...], 1e-6) * _silu(g_ref[:, vs:ve])).astype(mix_ref.dtype)


N_PROJ = 8
N_SCAN_PARAMS = 7


def _scan_body(*refs, c_len, n_steps, sps):
    proj = refs[:N_PROJ * sps]
    gla0_ref, ssd0_ref, conv0_ref = refs[N_PROJ * sps:N_PROJ * sps + 3]
    params = refs[N_PROJ * sps + 3:N_PROJ * sps + 3 + N_SCAN_PARAMS]
    mix_ref, gla_out_ref, ssd_out_ref, conv_out_ref, st_ref, ss_ref, xpad_ref, y_ref = \
        refs[N_PROJ * sps + 3 + N_SCAN_PARAMS:]
    j = pl.program_id(1)
    kc = SSD_CONV - 1
    conv_dim = xpad_ref.shape[2]

    @pl.when(j == 0)
    def _load_state():
        for s in range(sps):
            for h in range(GLA_HEADS):
                st_ref[s * GLA_HEADS + h] = gla0_ref[s, h].T
            ss_ref[s] = ssd0_ref[s]
            xpad_ref[s, 0:HIST, :] = jnp.zeros((HIST, conv_dim), F32)
            xpad_ref[s, HIST - kc:HIST, :] = conv0_ref[s]

    for s in range(sps):
        _scan_chunk(*proj[N_PROJ * s:N_PROJ * (s + 1)], *params, mix_ref.at[s],
                    st_ref.at[pl.ds(s * GLA_HEADS, GLA_HEADS)], ss_ref.at[s], xpad_ref.at[s], y_ref.at[s],
                    c_len=c_len)

    @pl.when(j == n_steps - 1)
    def _store_state():
        for s in range(sps):
            for h in range(GLA_HEADS):
                gla_out_ref[s, h] = st_ref[s * GLA_HEADS + h].T
            ssd_out_ref[s] = ss_ref[s]
            conv_out_ref[s] = xpad_ref[s, HIST - kc:HIST, :]


def _scan(proj, row0, n_seq, seq_len, c_len, gla0, ssd0, conv0, params):
    assert len(proj) == N_PROJ and len(params) == N_SCAN_PARAMS
    n_steps = seq_len // c_len
    blk0 = row0 // c_len
    conv_dim = proj[6].shape[1]
    mix_w = GLA_HEADS * GLA_DV + SSD_INNER
    sps = math.gcd(n_seq, SEQS_PER_STEP)

    def rows(w, s):
        return pl.BlockSpec((c_len, w), lambda b, j: (blk0 + (b * sps + s) * n_steps + j, 0))

    def per_seq(shape):
        nd = len(shape)
        return pl.BlockSpec((sps,) + tuple(shape[1:]), lambda b, j: (b,) + (0,) * (nd - 1))

    body = functools.partial(_scan_body, c_len=c_len, n_steps=n_steps, sps=sps)
    mix, gla, ssd, conv = pl.pallas_call(
        body,
        out_shape=[jax.ShapeDtypeStruct((n_seq, seq_len, mix_w), BF16 if c_len % (2 * SUBLANES) == 0 else F32),
                   jax.ShapeDtypeStruct(gla0.shape, F32),
                   jax.ShapeDtypeStruct(ssd0.shape, F32),
                   jax.ShapeDtypeStruct(conv0.shape, F32)],
        grid=(n_seq // sps, n_steps),
        in_specs=[rows(a.shape[1], s) for s in range(sps) for a in proj]
                 + [per_seq(gla0.shape), per_seq(ssd0.shape), per_seq(conv0.shape)]
                 + [_resident(p.shape) for p in params],
        out_specs=[pl.BlockSpec((sps, c_len, mix_w), lambda b, j: (b, j, 0)),
                   per_seq(gla0.shape), per_seq(ssd0.shape), per_seq(conv0.shape)],
        scratch_shapes=[pltpu.VMEM((sps * GLA_HEADS, GLA_DV, GLA_DK), F32),
                        pltpu.VMEM((sps, SSD_INNER, SSD_STATE), F32),
                        pltpu.VMEM((sps, HIST + c_len, conv_dim), F32),
                        pltpu.VMEM((sps, c_len, SSD_INNER), F32)],
        compiler_params=_cparams(2),
        name=f"scan_c{c_len}",
    )(*(list(proj) * sps), gla0, ssd0, conv0, *params)
    return mix.reshape(n_seq * seq_len, mix_w), gla, ssd, conv


def _softmax_step(blocks, m_ref, l_ref, acc_ref, shift=None):
    m_prev = m_ref[...]
    s_max = None
    for s, _ in blocks:
        bm = jnp.max(s, axis=1, keepdims=True)
        s_max = bm if s_max is None else jnp.maximum(s_max, bm)
    if shift is None:
        m_new = jnp.maximum(m_prev, s_max)
        off = m_new
    else:
        m_new = jnp.maximum(m_prev, s_max + shift)
        off = m_new - shift
    alpha = jnp.exp2(m_prev - m_new)
    l = alpha * l_ref[...]
    acc = alpha * acc_ref[...]
    for s, v in blocks:
        p = jnp.exp2(s - off[:, :1])
        l = l + jnp.sum(p, axis=1, keepdims=True)
        acc = acc + _dot(p.astype(BF16), v)
    l_ref[...] = l
    acc_ref[...] = acc
    m_ref[...] = m_new


def _softmax_step_t(blocks, m_ref, acc_ref, hh):
    m_prev = m_ref[hh]
    m_new = m_prev
    for s, _ in blocks:
        m_new = jnp.maximum(m_new, jnp.max(s, axis=0, keepdims=True))
    acc = jnp.exp2(m_prev - m_new) * acc_ref[hh]
    for s, vt1 in blocks:
        acc = acc + _dot(vt1, jnp.exp2(s - m_new).astype(BF16))
    acc_ref[hh] = acc
    m_ref[hh] = m_new


def _lambda(lamv_ref, lam_init):
    lv = lamv_ref[...]
    s1 = jnp.sum(lv[0:1, :] * lv[1:2, :], axis=1, keepdims=True)
    s2 = jnp.sum(lv[2:3, :] * lv[3:4, :], axis=1, keepdims=True)
    return jnp.exp(s1) - jnp.exp(s2) + lam_init


def _pad_rows(x, n):
    return jnp.concatenate([x, jnp.zeros((n - x.shape[0], x.shape[1]), x.dtype)], axis=0)


def _split_maps(qv):
    lane = lax.broadcasted_iota(jnp.int32, qv.shape, 1)
    return jnp.concatenate([jnp.where(lane < DA_DH, qv, 0.0), jnp.where(lane >= DA_DH, qv, 0.0)], axis=0)


ATT_HEADS_PER_STEP = 2
ONES_ROWS = 16
POS_RADIX = 64


def _attn_prompt_body(q_ref, qn_ref, k_ref, v_ref, km_ref, vm_ref, posk_ref, poskm_ref, qbias_ref, dmask_ref,
                      lamv_ref, subln_ref, o_ref, vt_ref, vtm_ref, q2_ref, m_ref, acc_ref, s_ref,
                      s0_ref, smax_ref, smax0_ref, *, tq, lam_init):
    qi = pl.program_id(2)
    seq = k_ref.shape[0]
    dv1 = DA_VD + ONES_ROWS
    cur = qi % 2

    def keys(r0, hh):
        return jnp.concatenate([k_ref[pl.ds(r0, tq), hh * DA_VD:(hh + 1) * DA_VD], posk_ref[pl.ds(r0, tq), :]],
                               axis=1)

    def score(r0, q_slot, hh, dst_ref, dmax_ref):
        s = _dot_nt(keys(r0, hh), q2_ref[q_slot, hh])
        dst_ref[hh] = s
        dmax_ref[hh] = jnp.max(s, axis=0, keepdims=True)

    def first_block(slot, src_ref, dst_ref, dmax_ref):
        for hh in range(ATT_HEADS_PER_STEP):
            q2_ref[slot, hh, :, 0:DA_VD] = _split_maps(src_ref[:, hh * DA_VD:(hh + 1) * DA_VD])
            q2_ref[slot, hh, :, DA_VD:2 * DA_VD] = jnp.broadcast_to(qbias_ref[hh], (2 * tq, LANES))
            score(0, slot, hh, dst_ref, dmax_ref)

    @pl.when(qi == 0)
    def _start_of_sequence():
        for hh in range(ATT_HEADS_PER_STEP):
            lanes = slice(hh * DA_VD, (hh + 1) * DA_VD)
            for c in range(seq // tq):
                vt_ref[hh, 0:DA_VD, c * tq:(c + 1) * tq] = v_ref[c * tq:(c + 1) * tq, lanes].T.astype(BF16)
            vt_ref[hh, DA_VD:dv1, :] = jnp.ones((ONES_ROWS, seq), BF16)
            vtm_ref[hh, 0:DA_VD, :] = _pad_rows(vm_ref[:, lanes], LANES).T.astype(BF16)
            vtm_ref[hh, DA_VD:dv1, :] = jnp.ones((ONES_ROWS, LANES), BF16)
        first_block(0, q_ref, s_ref.at[0], smax_ref.at[0])

    for hh in range(ATT_HEADS_PER_STEP):
        m_ref[hh] = jnp.full((1, 2 * tq), NEG_INF, F32)
        acc_ref[hh] = jnp.zeros((dv1, 2 * tq), F32)

    def produce(j, slot):
        r0 = pl.multiple_of(j * tq, tq)
        for hh in range(ATT_HEADS_PER_STEP):
            score(r0, cur, hh, s_ref.at[slot], smax_ref.at[slot])

    def consume(j, src_ref, max_ref):
        r0 = pl.multiple_of(j * tq, tq)
        for hh in range(ATT_HEADS_PER_STEP):
            m_prev = m_ref[hh]
            m_new = jnp.maximum(m_prev, max_ref[hh])
            p = jnp.exp2(src_ref[hh] - m_new).astype(BF16)
            acc_ref[hh] = jnp.exp2(m_prev - m_new) * acc_ref[hh] + _dot(vt_ref[hh, :, pl.ds(r0, tq)], p)
            m_ref[hh] = m_new

    @pl.when(qi >= 1)
    def _block_0():
        consume(0, s0_ref, smax0_ref)
        produce(1, 1)

    def two_blocks(t, carry):
        j = 2 * t + 1
        consume(j, s_ref.at[1], smax_ref.at[1])
        produce(j + 1, 0)
        consume(j + 1, s_ref.at[0], smax_ref.at[0])
        produce(j + 2, 1)
        return carry

    lax.fori_loop(0, jnp.maximum(qi - 1, 0) // 2, two_blocks, 0)

    @pl.when(jnp.logical_and(qi >= 2, cur == 0))
    def _odd_block():
        consume(qi - 1, s_ref.at[1], smax_ref.at[1])
        produce(qi, 0)

    r0 = pl.multiple_of(qi * tq, tq)
    lam = _lambda(lamv_ref, lam_init)
    first_block(1 - cur, qn_ref, s0_ref, smax0_ref)
    for hh in range(ATT_HEADS_PER_STEP):
        lanes = slice(hh * DA_VD, (hh + 1) * DA_VD)
        s_d = s_ref[cur, hh] + dmask_ref[...]
        n_meta = km_ref.shape[0]
        s_m = _dot_nt(jnp.concatenate([km_ref[:, lanes], poskm_ref[...]], axis=1), q2_ref[cur, hh])
        _softmax_step_t([(s_d, vt_ref[hh, :, pl.ds(r0, tq)]), (s_m, vtm_ref[hh, :, 0:n_meta])],
                        m_ref, acc_ref, hh)
        a = acc_ref[hh]
        on = a[0:DA_VD] * (1.0 / a[DA_VD:DA_VD + 1])
        o_t = on[:, :tq] - lam * on[:, tq:]
        o_ref[:, lanes] = (_rms(o_t.T, subln_ref[...], 1e-5) * (1.0 - lam_init)).astype(o_ref.dtype)


def _alibi_tables(seq_len, n_meta, tq):
    assert seq_len + n_meta <= POS_RADIX * 256
    slopes = jnp.exp2(-8.0 * jnp.arange(1, DA_HEADS + 1, dtype=F32) / DA_HEADS) * LOG2E
    hi, mid, lo = (t.astype(F32) for t in _split3(slopes))
    qbias = jnp.zeros((DA_HEADS, 1, LANES), F32)
    for i, t in enumerate((hi * POS_RADIX, mid * POS_RADIX, lo * POS_RADIX, hi, mid, lo)):
        qbias = qbias.at[:, 0, i].set(t)

    def digits(pos, n_rows):
        d = jnp.zeros((n_rows, LANES), F32)
        for i in range(3):
            d = d.at[:pos.shape[0], i].set((pos // POS_RADIX).astype(F32))
            d = d.at[:pos.shape[0], 3 + i].set((pos % POS_RADIX).astype(F32))
        return d.astype(BF16)

    posk = digits(n_meta + jnp.arange(seq_len), seq_len)
    poskm = digits(jnp.arange(n_meta), n_meta)
    key_t = jnp.arange(tq)[:, None]
    qry = jnp.arange(2 * tq)[None, :] % tq
    dmask = jnp.where(key_t <= qry, 0.0, NEG_INF).astype(F32)
    return posk, poskm, qbias.astype(BF16), dmask


def _attn_prompt(q, kb, v, kb_meta, v_meta, lamv, subln, n_seq, seq_len, n_meta, meta_row0, lam_init):
    tq = _row_tile(seq_len, 512)
    assert tq % LANES == 0
    nq = seq_len // tq
    hps = ATT_HEADS_PER_STEP
    wblk = hps * DA_VD
    meta_blk0 = meta_row0 // n_meta
    posk, poskm, qbias, dmask = _alibi_tables(seq_len, n_meta, tq)
    dv1 = DA_VD + ONES_ROWS
    body = functools.partial(_attn_prompt_body, tq=tq, lam_init=lam_init)
    const = lambda a: pl.BlockSpec(a.shape, lambda b, h, qi: (0,) * a.ndim, pipeline_mode=pl.Buffered(1))
    return pl.pallas_call(
        body,
        out_shape=jax.ShapeDtypeStruct((n_seq * seq_len, DA_HEADS * DA_VD), BF16),
        grid=(n_seq, DA_HEADS // hps, nq),
        in_specs=[pl.BlockSpec((tq, wblk), lambda b, h, qi: (b * nq + qi, h)),
                  pl.BlockSpec((tq, wblk), lambda b, h, qi: (b * nq + jnp.minimum(qi + 1, nq - 1), h)),
                  pl.BlockSpec((seq_len, wblk), lambda b, h, qi: (b, h)),
                  pl.BlockSpec((seq_len, wblk), lambda b, h, qi: (b, h)),
                  pl.BlockSpec((n_meta, wblk), lambda b, h, qi: (meta_blk0 + b, h)),
                  pl.BlockSpec((n_meta, wblk), lambda b, h, qi: (meta_blk0 + b, h)),
                  const(posk), const(poskm),
                  pl.BlockSpec((hps, 1, LANES), lambda b, h, qi: (h, 0, 0)),
                  const(dmask), const(lamv), const(subln)],
        out_specs=pl.BlockSpec((tq, wblk), lambda b, h, qi: (b * nq + qi, h)),
        scratch_shapes=[pltpu.VMEM((hps, dv1, seq_len), BF16), pltpu.VMEM((hps, dv1, LANES), BF16),
                        pltpu.VMEM((2, hps, 2 * tq, 2 * DA_VD), BF16), pltpu.VMEM((hps, 1, 2 * tq), F32),
                        pltpu.VMEM((hps, dv1, 2 * tq), F32), pltpu.VMEM((2, hps, tq, 2 * tq), F32),
                        pltpu.VMEM((hps, tq, 2 * tq), F32),
                        pltpu.VMEM((2, hps, 1, 2 * tq), F32), pltpu.VMEM((hps, 1, 2 * tq), F32)],
        compiler_params=_cparams(3),
        name="attn_prompt",
    )(q, q, kb, v, kb_meta, v_meta, posk, poskm, qbias, dmask, lamv, subln)


def _attn_meta_body(slopes_ref, q_ref, k_ref, v_ref, lamv_ref, subln_ref, o_ref,
                    m_ref, l_ref, acc_ref, *, n_meta, lam_init):
    h = pl.program_id(1)
    slope = slopes_ref[h]
    m_rows = 2 * n_meta
    m_ref[...] = jnp.full(m_ref.shape, NEG_INF, F32)
    l_ref[...] = jnp.zeros(l_ref.shape, F32)
    acc_ref[...] = jnp.zeros(acc_ref.shape, F32)
    qb = _split_maps(q_ref[...])
    r = lax.broadcasted_iota(jnp.int32, (m_rows, LANES), 0)
    r = jnp.where(r >= n_meta, r - n_meta, r)
    c = lax.broadcasted_iota(jnp.int32, (m_rows, LANES), 1)
    bias = jnp.where(c <= r, slope * c.astype(F32), NEG_INF)
    s = _dot_nt(qb, _pad_rows(k_ref[...], LANES).astype(BF16)) + bias
    _softmax_step([(s, _pad_rows(v_ref[...], LANES).astype(BF16))], m_ref, l_ref, acc_ref)
    on = acc_ref[...] / l_ref[...]
    o = on[:n_meta] - _lambda(lamv_ref, lam_init) * on[n_meta:]
    o_ref[...] = _rms(o, subln_ref[...], 1e-5) * (1.0 - lam_init)


def _attn_meta(q, k, v, slopes, lamv, subln, n_seq, n_meta, meta_row0, lam_init):
    blk0 = meta_row0 // n_meta
    spec = pl.BlockSpec((n_meta, DA_VD), lambda b, h: (blk0 + b, h))
    body = functools.partial(_attn_meta_body, n_meta=n_meta, lam_init=lam_init)
    return pl.pallas_call(
        body,
        out_shape=jax.ShapeDtypeStruct((n_seq * n_meta, DA_HEADS * DA_VD), F32),
        grid=(n_seq, DA_HEADS),
        in_specs=[pl.BlockSpec(memory_space=pltpu.SMEM), spec, spec, spec,
                  pl.BlockSpec(lamv.shape, lambda b, h: (0, 0)),
                  pl.BlockSpec(subln.shape, lambda b, h: (0, 0))],
        out_specs=pl.BlockSpec((n_meta, DA_VD), lambda b, h: (b, h)),
        scratch_shapes=[pltpu.VMEM((2 * n_meta, LANES), F32), pltpu.VMEM((2 * n_meta, LANES), F32),
                        pltpu.VMEM((2 * n_meta, DA_VD), F32)],
        compiler_params=_cparams(2),
        name="attn_meta",
    )(slopes, q, k, v, lamv, subln)


PAGES_PER_STEP = 8


def _attn_sample_body(pt_ref, q_ref, kn_ref, vn_ref, *rest, dec_len, n_pages, n_grp, lam_init):
    ck_refs, cv_refs = rest[:n_grp], rest[n_grp:2 * n_grp]
    (base_ref, basen_ref, rslope_ref, lamv_ref, subln_ref, o_ref,
     qb_ref, m_ref, l_ref, acc_ref, s_ref) = rest[2 * n_grp:]
    p = pl.program_id(1)
    rows_per_head = 2 * dec_len

    @pl.when(p == 0)
    def _start():
        lane = lax.broadcasted_iota(jnp.int32, (dec_len, DA_VD), 1)
        for h in range(DA_HEADS):
            qh = q_ref[:, h * DA_VD:(h + 1) * DA_VD]
            qb_ref[h * rows_per_head:h * rows_per_head + dec_len, :] = jnp.where(lane < DA_DH, qh, 0.0)
            qb_ref[h * rows_per_head + dec_len:(h + 1) * rows_per_head, :] = jnp.where(lane >= DA_DH, qh, 0.0)
        m_ref[...] = jnp.full(m_ref.shape, NEG_INF, F32)
        l_ref[...] = jnp.zeros(l_ref.shape, F32)
        acc_ref[...] = jnp.zeros(acc_ref.shape, F32)

    def head_rows(ref, h):
        return ref[0, pl.ds(h, PAGE, stride=DA_HEADS), :].astype(BF16)

    for u in range(n_grp):
        for h in range(DA_HEADS):
            hr = slice(h * rows_per_head, (h + 1) * rows_per_head)
            s_ref[hr, u * PAGE:(u + 1) * PAGE] = _dot_nt(qb_ref[hr, :], head_rows(ck_refs[u], h))
    s = s_ref[...] + base_ref[...]
    shift = rslope_ref[...] * ((p * n_grp - n_pages) * PAGE).astype(F32)
    m_prev = m_ref[...]
    m_new = jnp.maximum(m_prev, jnp.max(s, axis=1, keepdims=True) + shift)
    pm = jnp.exp2(s - (m_new - shift)[:, :1])
    alpha = jnp.exp2(m_prev - m_new)
    l_ref[...] = alpha * l_ref[...] + jnp.sum(pm, axis=1, keepdims=True)
    m_ref[...] = m_new
    pb = pm.astype(BF16)
    for h in range(DA_HEADS):
        hr = slice(h * rows_per_head, (h + 1) * rows_per_head)
        acc = alpha[hr] * acc_ref[hr, :]
        for u in range(n_grp):
            acc = acc + _dot(pb[hr, u * PAGE:(u + 1) * PAGE], head_rows(cv_refs[u], h))
        acc_ref[hr, :] = acc

    @pl.when(p == n_pages // n_grp - 1)
    def _finish():
        s = _dot_nt(qb_ref[...], _pad_rows(kn_ref[0], LANES).astype(BF16)) + basen_ref[...]
        _softmax_step([(s, _pad_rows(vn_ref[0], LANES).astype(BF16))], m_ref, l_ref, acc_ref)
        on = acc_ref[...] / l_ref[...]
        lam = _lambda(lamv_ref, lam_init)
        for h in range(DA_HEADS):
            o = (on[h * rows_per_head:h * rows_per_head + dec_len]
                 - lam * on[h * rows_per_head + dec_len:(h + 1) * rows_per_head])
            o_ref[:, h * DA_VD:(h + 1) * DA_VD] = _rms(o, subln_ref[...], 1e-5) * (1.0 - lam_init)


def _attn_sample(q, kn, vn, ck, cv, page_table, lamv, subln, n_seq, dec_len, samp_row0, lam_init):
    n_pages = page_table.shape[1]
    n_grp = math.gcd(n_pages, PAGES_PER_STEP)
    rows = DA_HEADS * 2 * dec_len
    assert rows == LANES and DA_HEADS * dec_len <= LANES
    page_rows = PAGE * DA_HEADS
    slopes = 2.0 ** (-8.0 * (jnp.arange(rows) // (2 * dec_len) + 1) / DA_HEADS) * LOG2E
    rhead = jnp.arange(rows) // (2 * dec_len)
    rq = jnp.arange(rows) % dec_len
    base = slopes[:, None] * jnp.arange(n_grp * PAGE)[None, :].astype(F32)
    cn = jnp.arange(LANES)
    valid = (cn[None, :] < DA_HEADS * dec_len) & (rhead[:, None] == (cn % DA_HEADS)[None, :]) \
        & ((cn // DA_HEADS)[None, :] <= rq[:, None])
    basen = jnp.where(valid, slopes[:, None] * (cn // DA_HEADS)[None, :].astype(F32), NEG_INF)
    rslope = jnp.broadcast_to(slopes[:, None], (rows, LANES)).astype(F32)
    blk0 = samp_row0 // dec_len
    body = functools.partial(_attn_sample_body, dec_len=dec_len, n_pages=n_pages, n_grp=n_grp, lam_init=lam_init)
    width = DA_HEADS * DA_VD

    def page(u):
        return pl.BlockSpec((1, page_rows, DA_VD), lambda i, p, pt: (pt[i, p * n_grp + u], 0, 0))

    const = lambda a: pl.BlockSpec(a.shape, lambda i, p, pt: (0,) * a.ndim)
    grid_spec = pltpu.PrefetchScalarGridSpec(
        num_scalar_prefetch=1,
        grid=(n_seq, n_pages // n_grp),
        in_specs=[pl.BlockSpec((dec_len, width), lambda i, p, pt: (blk0 + i, 0)),
                  pl.BlockSpec((1,) + kn.shape[1:], lambda i, p, pt: (i, 0, 0)),
                  pl.BlockSpec((1,) + vn.shape[1:], lambda i, p, pt: (i, 0, 0))]
                 + [page(u) for u in range(n_grp)] * 2
                 + [const(base), const(basen), const(rslope), const(lamv), const(subln)],
        out_specs=pl.BlockSpec((dec_len, width), lambda i, p, pt: (i, 0)),
        scratch_shapes=[pltpu.VMEM((rows, DA_VD), BF16), pltpu.VMEM((rows, LANES), F32),
                        pltpu.VMEM((rows, LANES), F32), pltpu.VMEM((rows, DA_VD), F32),
                        pltpu.VMEM((rows, n_grp * PAGE), F32)])
    return pl.pallas_call(
        body,
        out_shape=jax.ShapeDtypeStruct((n_seq * dec_len, width), F32),
        grid_spec=grid_spec,
        compiler_params=_cparams(2),
        name="attn_sample",
    )(page_table, q, kn, vn, *([ck] * n_grp), *([cv] * n_grp), base.astype(F32), basen.astype(F32), rslope,
      lamv, subln)


def kernel(x_prompt, x_sample, state_gla, state_ssd, state_conv, cache_k, cache_v, page_table, meta_tokens, norm_ffn1, w_ffn1_gu, w_ffn1_down, norm_mix, norm_ffn2, w_ffn2_gu, w_ffn2_down, w_in_ab, gla_gk_w2, gla_gk_b, gla_norm, ssd_conv_w, ssd_conv_b, ssd_dt_bias, ssd_a_log, ssd_d, ssd_norm, w_out_ab, w_qkv_c, lam_q1, lam_k1, lam_q2, lam_k2, subln_c, w_out_c, norm_final):
    n_b, seq, d = x_prompt.shape
    n_s, dec, _ = x_sample.shape
    n_meta = meta_tokens.shape[0]
    depth = norm_ffn1.shape[0]
    assert seq % CHUNK == 0 and n_meta % SUBLANES == 0 and dec % SUBLANES == 0
    main_rows, meta_rows, samp_rows = n_b * seq, n_b * n_meta, n_s * dec
    meta_row0, samp_row0 = main_rows, main_rows + meta_rows

    x = jnp.concatenate([x_prompt.reshape(main_rows, d),
                         jnp.tile(meta_tokens.astype(F32), (n_b, 1)),
                         x_sample.reshape(samp_rows, d)], axis=0)

    row = lambda a: a.reshape(1, -1).astype(F32)
    gla_p, gla_s, ssd_p, ssd_s, conv_p, conv_s = [], [], [], [], [], []
    k_p, v_p, k_s, v_s = [], [], [], []
    conv_dim = ssd_conv_w.shape[2]
    gqk, gv = GLA_HEADS * GLA_DK, GLA_HEADS * GLA_DV
    bounds = [0]
    for w in (gqk, gqk, gv, GLA_RANK, gv, SSD_INNER, conv_dim, SSD_HEADS):
        bounds.append(bounds[-1] + w)
    expand = (jnp.arange(SSD_INNER)[None, :] // SSD_HEADDIM == jnp.arange(SSD_HEADS)[:, None]).astype(BF16)

    for l in range(depth):
        x = _ffn(x, row(norm_ffn1[l]), *_ffn_weights(w_ffn1_gu[l], w_ffn1_down[l]), row(norm_final), False)
        if l % 2 == 0:
            e = l // 2
            ws = [w_in_ab[e][:, bounds[i]:bounds[i + 1]].astype(BF16) for i in range(8)]
            proj = _inproj_ab(x, row(norm_mix[l]), ws, gla_gk_w2[e].astype(BF16), row(gla_gk_b[e]),
                              row(ssd_dt_bias[e]))
            params = (row(gla_norm[e]), ssd_conv_w[e].astype(F32), row(ssd_conv_b[e]), row(ssd_a_log[e]),
                      row(jnp.repeat(ssd_d[e], SSD_HEADDIM)), row(ssd_norm[e]), expand)
            ssd_shape = (SSD_INNER, SSD_STATE)
            zg = jnp.zeros((n_b, GLA_HEADS, GLA_DK, GLA_DV), F32)
            zs = jnp.zeros((n_b,) + ssd_shape, F32)
            zc = jnp.zeros((n_b, SSD_CONV - 1, conv_dim), F32)
            mix_meta, sg, ss, cv = _scan(proj, meta_row0, n_b, n_meta, n_meta, zg, zs, zc, params)
            mix_main, sg, ss, cv = _scan(proj, 0, n_b, seq, CHUNK, sg, ss, cv, params)
            gla_p.append(sg)
            ssd_p.append(ss.reshape(n_b, SSD_HEADS, SSD_HEADDIM, SSD_STATE))
            conv_p.append(cv)
            mix_samp, sg, ss, cv = _scan(proj, samp_row0, n_s, dec, dec, state_gla[e].astype(F32),
                                         state_ssd[e].astype(F32).reshape((n_s,) + ssd_shape),
                                         state_conv[e].astype(F32), params)
            gla_s.append(sg)
            ssd_s.append(ss.reshape(n_s, SSD_HEADS, SSD_HEADDIM, SSD_STATE))
            conv_s.append(cv)
            w_out = w_out_ab[e].astype(BF16)
            for mix, row0 in ((mix_main, 0), (mix_meta, meta_row0), (mix_samp, samp_row0)):
                x = _outproj(x, mix, w_out, row0)
        else:
            o = l // 2
            lam_init = 0.8 - 0.6 * math.exp(-0.3 * l)
            width = DA_HEADS * DA_VD
            wq, wk, wv = (w_qkv_c[o][:, i * width:(i + 1) * width].astype(BF16) for i in range(3))
            q, k, v, kb = _qkv(x, row(norm_mix[l]), wq, wk, wv, 0, main_rows)
            q_r, k_r, v_r, kb_r = _qkv(x, row(norm_mix[l]), wq, wk, wv, main_rows, meta_rows + samp_rows)
            slopes = jnp.exp2(-8.0 * jnp.arange(1, DA_HEADS + 1, dtype=F32) / DA_HEADS) * LOG2E
            lamv = jnp.stack([lam_q1[o], lam_k1[o], lam_q2[o], lam_k2[o]]).astype(F32)
            subln = row(subln_c[o])
            o_main = _attn_prompt(q, kb, v, kb_r, v_r, lamv, subln, n_b, seq, n_meta, 0, lam_init)
            o_meta = _attn_meta(q_r, k_r, v_r, slopes, lamv, subln, n_b, n_meta, 0, lam_init)
            kn = k_r[meta_rows:].reshape(n_s, dec * DA_HEADS, DA_VD)
            vn = v_r[meta_rows:].reshape(n_s, dec * DA_HEADS, DA_VD)
            ck = cache_k[o].reshape(cache_k.shape[1], PAGE * DA_HEADS, DA_VD)
            cv_ = cache_v[o].reshape(cache_v.shape[1], PAGE * DA_HEADS, DA_VD)
            o_samp = _attn_sample(q_r, kn, vn, ck, cv_, page_table, lamv, subln, n_s, dec, meta_rows, lam_init)
            w_out = w_out_c[o].astype(BF16)
            for att, row0 in ((o_main, 0), (o_meta, meta_row0), (o_samp, samp_row0)):
                x = _outproj(x, att, w_out, row0)
            rows5 = lambda a, n, t: a.reshape(n, t, DA_HEADS, DA_VD)
            k_p.append(jnp.concatenate([rows5(k_r[:meta_rows], n_b, n_meta), rows5(k, n_b, seq)], axis=1))
            v_p.append(jnp.concatenate([rows5(v_r[:meta_rows], n_b, n_meta), rows5(v, n_b, seq)], axis=1))
            k_s.append(rows5(k_r[meta_rows:], n_s, dec))
            v_s.append(rows5(v_r[meta_rows:], n_s, dec))
        ffn2 = (row(norm_ffn2[l]), *_ffn_weights(w_ffn2_gu[l], w_ffn2_down[l]), row(norm_final))
        if l < depth - 1:
            x = _ffn(x, *ffn2, False)

    y_main = _ffn(x, *ffn2, True, 0, main_rows, 1024)
    y_rest = _ffn(x, *ffn2, True, main_rows, meta_rows + samp_rows, 1024)
    y_prompt = y_main.reshape(n_b, seq, d)
    y_sample = y_rest[meta_rows:].reshape(n_s, dec, d)
    return (y_prompt, y_sample, jnp.stack(gla_p), jnp.stack(gla_s), jnp.stack(ssd_p), jnp.stack(ssd_s),
            jnp.stack(conv_p), jnp.stack(conv_s), jnp.stack(k_p), jnp.stack(v_p), jnp.stack(k_s), jnp.stack(v_s))
```

```python
import functools
import math

import jax
import jax.numpy as jnp
from jax import lax
from jax.experimental import pallas as pl
from jax.experimental.pallas import tpu as pltpu

F32 = jnp.float32
BF16 = jnp.bfloat16

CHUNK = 64
GLA_HEADS, GLA_DK, GLA_DV, GLA_RANK = 4, 128, 256, 16
GLA_GATE_NORM = 16.0
SSD_HEADS, SSD_HEADDIM, SSD_STATE, SSD_GROUPS, SSD_CONV = 16, 64, 64, 2, 4
SSD_INNER = SSD_HEADS * SSD_HEADDIM
DA_HEADS, DA_DH = 8, 64
DA_VD = 2 * DA_DH
PAGE = 128

LANES = 128
SUBLANES = 8
VMEM_LIMIT = 56 * 1024 * 1024

NEG_INF = float("-inf")


def _cparams(n_axes):
    return pltpu.CompilerParams(dimension_semantics=("arbitrary",) * n_axes,
                                vmem_limit_bytes=VMEM_LIMIT)


def _row_tile(n, target):
    best = SUBLANES
    for t in range(SUBLANES, target + 1, SUBLANES):
        if n % t == 0:
            best = t
    return best


def _resident(shape):
    nd = len(shape)
    return pl.BlockSpec(shape, lambda *_: (0,) * nd, pipeline_mode=pl.Buffered(1))


def _rms(x, w, eps):
    ms = jnp.mean(x * x, axis=-1, keepdims=True)
    return x * lax.rsqrt(ms + eps) * w


def _silu(x):
    return x * jax.nn.sigmoid(x)


def _softplus(x):
    return jnp.maximum(x, 0.0) + jnp.log1p(jnp.exp(-jnp.abs(x)))


def _dot(a, b):
    return jnp.dot(a, b, preferred_element_type=F32)


def _dot_nt(a, b):
    return lax.dot_general(a, b, (((1,), (1,)), ((), ())), preferred_element_type=F32)


def _dot_tn(a, b):
    return lax.dot_general(a, b, (((0,), (0,)), ((), ())), preferred_element_type=F32)


def _split3(a):
    hi = a.astype(BF16)
    r1 = a - hi.astype(F32)
    mid = r1.astype(BF16)
    lo = (r1 - mid.astype(F32)).astype(BF16)
    return hi, mid, lo


def _dot01_left(m01, a):
    hi, mid, lo = _split3(a)
    return _dot(m01, hi) + _dot(m01, mid) + _dot(m01, lo)


def _dot01_right(a, m01):
    hi, mid, lo = _split3(a)
    return _dot(hi, m01) + _dot(mid, m01) + _dot(lo, m01)


FFN_CHUNK = 256


def _ffn_body(x_ref, nw_ref, wgu_ref, wd_ref, fw_ref, o_ref, *, final_norm):
    x = x_ref[...]
    xn = _rms(x, nw_ref[...], 1e-6).astype(BF16)
    ff = wd_ref.shape[0]
    acc = None
    for c0 in range(0, ff, FFN_CHUNK):
        g = _dot(xn, wgu_ref[:, c0:c0 + FFN_CHUNK])
        u = _dot(xn, wgu_ref[:, ff + c0:ff + c0 + FFN_CHUNK])
        down = _dot((_silu(g) * u).astype(BF16), wd_ref[c0:c0 + FFN_CHUNK, :])
        acc = down if acc is None else acc + down
    y = x + 0.5 * acc
    if final_norm:
        y = _rms(y, fw_ref[...], 1e-6)
    o_ref[...] = y


def _ffn(x, nw, wgu, wd, fw, final_norm, row0=0, n_rows=None, tile=896):
    n, d = x.shape
    n_rows = n if n_rows is None else n_rows
    assert wd.shape[0] % FFN_CHUNK == 0 and wgu.shape[1] == 2 * wd.shape[0]
    tm = _row_tile(math.gcd(n_rows, row0) if row0 else n_rows, tile)
    blk0 = row0 // tm
    body = functools.partial(_ffn_body, final_norm=final_norm)
    return pl.pallas_call(
        body,
        out_shape=jax.ShapeDtypeStruct((n_rows, d), F32),
        grid=(n_rows // tm,),
        in_specs=[pl.BlockSpec((tm, d), lambda i: (blk0 + i, 0)),
                  _resident(nw.shape), _resident(wgu.shape), _resident(wd.shape), _resident(fw.shape)],
        out_specs=pl.BlockSpec((tm, d), lambda i: (i, 0)),
        compiler_params=_cparams(1),
        name="ffn",
    )(x, nw, wgu, wd, fw)


def _ffn_weights(w_gu, w_down):
    return w_gu.astype(BF16), w_down.astype(BF16)


def _inproj_ab_body(x_ref, nw_ref, wq_ref, wk_ref, wv_ref, wgk_ref, wg_ref, wz_ref, wx_ref, wdt_ref,
                    gk2_ref, gkb_ref, dtb_ref,
                    q_ref, k_ref, v_ref, lf_ref, g_ref, z_ref, xbc_ref, dt_ref):
    xn = _rms(x_ref[...], nw_ref[...], 1e-6).astype(BF16)
    q_ref[...] = _dot(xn, wq_ref[...]) * (GLA_DK ** -0.5)
    k_ref[...] = _dot(xn, wk_ref[...])
    v_ref[...] = _dot(xn, wv_ref[...])
    gk_lo = _dot(xn, wgk_ref[...]).astype(BF16)
    gk = _dot(gk_lo, gk2_ref[...]) + gkb_ref[...]
    lf_ref[...] = -_softplus(-gk) / GLA_GATE_NORM
    g_ref[...] = _dot(xn, wg_ref[...])
    z_ref[...] = _dot(xn, wz_ref[...])
    xbc_ref[...] = _dot(xn, wx_ref[...])
    dt_ref[...] = _softplus(_dot(xn, wdt_ref[...]) + dtb_ref[...])


def _inproj_ab(x, nw, ws, gk2, gkb, dtb):
    n, d = x.shape
    tm = _row_tile(n, 448)
    widths = [w.shape[1] for w in ws]
    out_w = [widths[0], widths[1], widths[2], gk2.shape[1], widths[4], widths[5], widths[6], widths[7]]
    return pl.pallas_call(
        _inproj_ab_body,
        out_shape=[jax.ShapeDtypeStruct((n, w), F32) for w in out_w],
        grid=(n // tm,),
        in_specs=[pl.BlockSpec((tm, d), lambda i: (i, 0)), _resident(nw.shape)]
                 + [_resident(w.shape) for w in ws]
                 + [_resident(gk2.shape), _resident(gkb.shape), _resident(dtb.shape)],
        out_specs=[pl.BlockSpec((tm, w), lambda i: (i, 0)) for w in out_w],
        compiler_params=_cparams(1),
        name="inproj_ab",
    )(x, nw, *ws, gk2, gkb, dtb)


LOG2E = 1.4426950408889634


def _qkv_body(x_ref, nw_ref, wq_ref, wk_ref, wv_ref, q_ref, k_ref, v_ref, kb_ref):
    xn = _rms(x_ref[...], nw_ref[...], 1e-6).astype(BF16)
    q_ref[...] = (_dot(xn, wq_ref[...]) * (DA_DH ** -0.5 * LOG2E)).astype(BF16)
    k = _dot(xn, wk_ref[...])
    k_ref[...] = k
    kb_ref[...] = k.astype(BF16)
    v_ref[...] = _dot(xn, wv_ref[...])


def _qkv(x, nw, wq, wk, wv, row0, n_rows):
    d = x.shape[1]
    tm = _row_tile(math.gcd(n_rows, row0) if row0 else n_rows, 1024)
    blk0 = row0 // tm
    w = wq.shape[1]
    return pl.pallas_call(
        _qkv_body,
        out_shape=[jax.ShapeDtypeStruct((n_rows, w), BF16), jax.ShapeDtypeStruct((n_rows, w), F32),
                   jax.ShapeDtypeStruct((n_rows, w), F32), jax.ShapeDtypeStruct((n_rows, w), BF16)],
        grid=(n_rows // tm,),
        in_specs=[pl.BlockSpec((tm, d), lambda i: (blk0 + i, 0)), _resident(nw.shape),
                  _resident(wq.shape), _resident(wk.shape), _resident(wv.shape)],
        out_specs=[pl.BlockSpec((tm, w), lambda i: (i, 0))] * 4,
        compiler_params=_cparams(1),
        name="qkv",
    )(x, nw, wq, wk, wv)


def _outproj_body(x_ref, m_ref, w_ref, o_ref):
    o_ref[...] = x_ref[...] + _dot(m_ref[...].astype(BF16), w_ref[...])


def _outproj(x, mix, w, row0):
    n, d = x.shape
    ns, km = mix.shape
    tm = _row_tile(math.gcd(ns, row0) if row0 else ns, 1024)
    blk0 = row0 // tm
    return pl.pallas_call(
        _outproj_body,
        out_shape=jax.ShapeDtypeStruct((n, d), F32),
        grid=(ns // tm,),
        in_specs=[pl.BlockSpec((tm, d), lambda i: (blk0 + i, 0)), pl.BlockSpec((tm, km), lambda i: (i, 0)),
                  _resident(w.shape)],
        out_specs=pl.BlockSpec((tm, d), lambda i: (blk0 + i, 0)),
        input_output_aliases={0: 0},
        compiler_params=_cparams(1),
        name="outproj",
    )(x, mix, w)


HIST = SUBLANES


SEQS_PER_STEP = 2


def _scan_chunk(q_ref, k_ref, v_ref, lf_ref, g_ref, z_ref, xbc_ref, dt_ref,
                gnorm_ref, cw_ref, cb_ref, alog_ref, dskip_ref, snorm_ref, expand_ref,
                mix_ref, st_ref, ss_ref, xpad_ref, y_ref, *, c_len):
    kc = SSD_CONV - 1

    row = lax.broadcasted_iota(jnp.int32, (c_len, c_len), 0)
    col = lax.broadcasted_iota(jnp.int32, (c_len, c_len), 1)
    tril = row >= col
    ltri = jnp.where(tril, 1.0, 0.0).astype(BF16)

    xpad_ref[HIST:HIST + c_len, :] = xbc_ref[...]
    acc = cb_ref[...] + xpad_ref[HIST - kc:HIST - kc + c_len, :] * cw_ref[0:1, :]
    for t in range(1, SSD_CONV):
        acc = acc + xpad_ref[HIST - kc + t:HIST - kc + t + c_len, :] * cw_ref[t:t + 1, :]
    tail = xpad_ref[HIST + c_len - kc:HIST + c_len, :]
    xpad_ref[HIST - kc:HIST, :] = tail
    xc = _silu(acc)
    xs = xc[:, :SSD_INNER]
    gs = SSD_GROUPS * SSD_STATE
    bmat = xc[:, SSD_INNER:SSD_INNER + gs]
    cmat = xc[:, SSD_INNER + gs:SSD_INNER + 2 * gs]

    expand = expand_ref[...]
    dt = dt_ref[...]
    da = dt * (-jnp.exp(alog_ref[...]))
    cs = _dot01_left(ltri, da)
    cs_t = cs.T
    cs_x = _dot01_right(cs, expand)
    dt_x = _dot01_right(dt, expand)
    cs_last_x = cs_x[c_len - 1:c_len, :]
    xdt = xs * dt_x
    e_cs = jnp.exp(cs_x)
    xw = xdt * jnp.exp(cs_last_x - cs_x)
    g_last = jnp.exp(cs_last_x)
    hpg = SSD_HEADS // SSD_GROUPS
    gw = hpg * SSD_HEADDIM
    for gi in range(SSD_GROUPS):
        bg = bmat[:, gi * SSD_STATE:(gi + 1) * SSD_STATE].astype(BF16)
        cg = cmat[:, gi * SSD_STATE:(gi + 1) * SSD_STATE].astype(BF16)
        gmat = _dot_nt(cg, bg)
        s_g = ss_ref[gi * gw:(gi + 1) * gw, :]
        y_inter = _dot_nt(cg, s_g.astype(BF16)) * e_cs[:, gi * gw:(gi + 1) * gw]
        ds_g = _dot_tn(xw[:, gi * gw:(gi + 1) * gw].astype(BF16), bg)
        for hh in range(hpg):
            h = gi * hpg + hh
            lo, hi = h * SSD_HEADDIM, (h + 1) * SSD_HEADDIM
            seg = jnp.broadcast_to(cs[:, h:h + 1], (c_len, c_len)) - cs_t[h:h + 1, :]
            decay = jnp.exp(jnp.where(tril, seg, NEG_INF))
            sc = (gmat * decay).astype(BF16)
            y_intra = _dot(sc, xdt[:, lo:hi].astype(BF16))
            y_ref[:, lo:hi] = y_intra + y_inter[:, hh * SSD_HEADDIM:(hh + 1) * SSD_HEADDIM]
            ss_ref[lo:hi, :] = (g_last[:, lo:hi] * s_g[hh * SSD_HEADDIM:(hh + 1) * SSD_HEADDIM, :]
                                + ds_g[hh * SSD_HEADDIM:(hh + 1) * SSD_HEADDIM, :])
    y = (y_ref[...] + xs * dskip_ref[...]) * _silu(z_ref[...])
    half = SSD_INNER // SSD_GROUPS
    for gi in range(SSD_GROUPS):
        mix_ref[:, GLA_HEADS * GLA_DV + gi * half:GLA_HEADS * GLA_DV + (gi + 1) * half] = _rms(
            y[:, gi * half:(gi + 1) * half], snorm_ref[:, gi * half:(gi + 1) * half], 1e-6).astype(mix_ref.dtype)

    b = _dot01_left(ltri, lf_ref[...])
    b_last = b[c_len - 1:c_len, :]
    q_dec = q_ref[...] * jnp.exp(b)
    k = k_ref[...]
    k_inv = k * jnp.exp(-b)
    k_end = k * jnp.exp(b_last - b)
    g_row = jnp.exp(b_last)
    for h in range(GLA_HEADS):
        ks, ke = h * GLA_DK, (h + 1) * GLA_DK
        vs, ve = h * GLA_DV, (h + 1) * GLA_DV
        qd = q_dec[:, ks:ke].astype(BF16)
        att = jnp.where(tril, _dot_nt(qd, k_inv[:, ks:ke].astype(BF16)), 0.0)
        v_h = v_ref[:, vs:ve].astype(BF16)
        st = st_ref[h]
        o = _dot(att.astype(BF16), v_h) + _dot_nt(qd, st.astype(BF16))
        st_ref[h] = g_row[:, ks:ke] * st + _dot_tn(v_h, k_end[:, ks:ke].astype(BF16))
        mix_ref[:, vs:ve] = (_rms(o, gnorm_ref[...], 1e-6) * _silu(g_ref[:, vs:ve])).astype(mix_ref.dtype)


N_PROJ = 8
N_SCAN_PARAMS = 7


def _scan_body(*refs, c_len, n_steps, sps):
    proj = refs[:N_PROJ * sps]
    gla0_ref, ssd0_ref, conv0_ref = refs[N_PROJ * sps:N_PROJ * sps + 3]
    params = refs[N_PROJ * sps + 3:N_PROJ * sps + 3 + N_SCAN_PARAMS]
    mix_ref, gla_out_ref, ssd_out_ref, conv_out_ref, st_ref, ss_ref, xpad_ref, y_ref = \
        refs[N_PROJ * sps + 3 + N_SCAN_PARAMS:]
    j = pl.program_id(1)
    kc = SSD_CONV - 1
    conv_dim = xpad_ref.shape[2]

    @pl.when(j == 0)
    def _load_state():
        for s in range(sps):
            for h in range(GLA_HEADS):
                st_ref[s * GLA_HEADS + h] = gla0_ref[s, h].T
            ss_ref[s] = ssd0_ref[s]
            xpad_ref[s, 0:HIST, :] = jnp.zeros((HIST, conv_dim), F32)
            xpad_ref[s, HIST - kc:HIST, :] = conv0_ref[s]

    for s in range(sps):
        _scan_chunk(*proj[N_PROJ * s:N_PROJ * (s + 1)], *params, mix_ref.at[s],
                    st_ref.at[pl.ds(s * GLA_HEADS, GLA_HEADS)], ss_ref.at[s], xpad_ref.at[s], y_ref.at[s],
                    c_len=c_len)

    @pl.when(j == n_steps - 1)
    def _store_state():
        for s in range(sps):
            for h in range(GLA_HEADS):
                gla_out_ref[s, h] = st_ref[s * GLA_HEADS + h].T
            ssd_out_ref[s] = ss_ref[s]
            conv_out_ref[s] = xpad_ref[s, HIST - kc:HIST, :]


def _scan(proj, row0, n_seq, seq_len, c_len, gla0, ssd0, conv0, params):
    assert len(proj) == N_PROJ and len(params) == N_SCAN_PARAMS
    n_steps = seq_len // c_len
    blk0 = row0 // c_len
    conv_dim = proj[6].shape[1]
    mix_w = GLA_HEADS * GLA_DV + SSD_INNER
    sps = math.gcd(n_seq, SEQS_PER_STEP)

    def rows(w, s):
        return pl.BlockSpec((c_len, w), lambda b, j: (blk0 + (b * sps + s) * n_steps + j, 0))

    def per_seq(shape):
        nd = len(shape)
        return pl.BlockSpec((sps,) + tuple(shape[1:]), lambda b, j: (b,) + (0,) * (nd - 1))

    body = functools.partial(_scan_body, c_len=c_len, n_steps=n_steps, sps=sps)
    mix, gla, ssd, conv = pl.pallas_call(
        body,
        out_shape=[jax.ShapeDtypeStruct((n_seq, seq_len, mix_w), BF16 if c_len % (2 * SUBLANES) == 0 else F32),
                   jax.ShapeDtypeStruct(gla0.shape, F32),
                   jax.ShapeDtypeStruct(ssd0.shape, F32),
                   jax.ShapeDtypeStruct(conv0.shape, F32)],
        grid=(n_seq // sps, n_steps),
        in_specs=[rows(a.shape[1], s) for s in range(sps) for a in proj]
                 + [per_seq(gla0.shape), per_seq(ssd0.shape), per_seq(conv0.shape)]
                 + [_resident(p.shape) for p in params],
        out_specs=[pl.BlockSpec((sps, c_len, mix_w), lambda b, j: (b, j, 0)),
                   per_seq(gla0.shape), per_seq(ssd0.shape), per_seq(conv0.shape)],
        scratch_shapes=[pltpu.VMEM((sps * GLA_HEADS, GLA_DV, GLA_DK), F32),
                        pltpu.VMEM((sps, SSD_INNER, SSD_STATE), F32),
                        pltpu.VMEM((sps, HIST + c_len, conv_dim), F32),
                        pltpu.VMEM((sps, c_len, SSD_INNER), F32)],
        compiler_params=_cparams(2),
        name=f"scan_c{c_len}",
    )(*(list(proj) * sps), gla0, ssd0, conv0, *params)
    return mix.reshape(n_seq * seq_len, mix_w), gla, ssd, conv


def _softmax_step(blocks, m_ref, l_ref, acc_ref, shift=None):
    m_prev = m_ref[...]
    s_max = None
    for s, _ in blocks:
        bm = jnp.max(s, axis=1, keepdims=True)
        s_max = bm if s_max is None else jnp.maximum(s_max, bm)
    if shift is None:
        m_new = jnp.maximum(m_prev, s_max)
        off = m_new
    else:
        m_new = jnp.maximum(m_prev, s_max + shift)
        off = m_new - shift
    alpha = jnp.exp2(m_prev - m_new)
    l = alpha * l_ref[...]
    acc = alpha * acc_ref[...]
    for s, v in blocks:
        p = jnp.exp2(s - off[:, :1])
        l = l + jnp.sum(p, axis=1, keepdims=True)
        acc = acc + _dot(p.astype(BF16), v)
    l_ref[...] = l
    acc_ref[...] = acc
    m_ref[...] = m_new


def _softmax_step_t(blocks, m_ref, acc_ref, hh):
    m_prev = m_ref[hh]
    m_new = m_prev
    for s, _ in blocks:
        m_new = jnp.maximum(m_new, jnp.max(s, axis=0, keepdims=True))
    acc = jnp.exp2(m_prev - m_new) * acc_ref[hh]
    for s, vt1 in blocks:
        acc = acc + _dot(vt1, jnp.exp2(s - m_new).astype(BF16))
    acc_ref[hh] = acc
    m_ref[hh] = m_new


def _lambda(lamv_ref, lam_init):
    lv = lamv_ref[...]
    s1 = jnp.sum(lv[0:1, :] * lv[1:2, :], axis=1, keepdims=True)
    s2 = jnp.sum(lv[2:3, :] * lv[3:4, :], axis=1, keepdims=True)
    return jnp.exp(s1) - jnp.exp(s2) + lam_init


def _pad_rows(x, n):
    return jnp.concatenate([x, jnp.zeros((n - x.shape[0], x.shape[1]), x.dtype)], axis=0)


def _split_maps(qv):
    lane = lax.broadcasted_iota(jnp.int32, qv.shape, 1)
    return jnp.concatenate([jnp.where(lane < DA_DH, qv, 0.0), jnp.where(lane >= DA_DH, qv, 0.0)], axis=0)


ATT_HEADS_PER_STEP = 2
ONES_ROWS = 16
POS_RADIX = 64


def _attn_prompt_body(q_ref, qn_ref, k_ref, v_ref, km_ref, vm_ref, posk_ref, poskm_ref, qbias_ref, dmask_ref,
                      lamv_ref, subln_ref, o_ref, vt_ref, vtm_ref, q2_ref, m_ref, acc_ref, s_ref,
                      s0_ref, smax_ref, smax0_ref, *, tq, lam_init):
    qi = pl.program_id(2)
    seq = k_ref.shape[0]
    dv1 = DA_VD + ONES_ROWS
    cur = qi % 2

    def keys(r0, hh):
        return jnp.concatenate([k_ref[pl.ds(r0, tq), hh * DA_VD:(hh + 1) * DA_VD], posk_ref[pl.ds(r0, tq), :]],
                               axis=1)

    def score(r0, q_slot, hh, dst_ref, dmax_ref, causal=False):
        s = _dot_nt(keys(r0, hh), q2_ref[q_slot, hh])
        if causal:
            s = s + dmask_ref[...]
        dst_ref[hh] = s
        dmax_ref[hh] = jnp.max(s, axis=0, keepdims=True)

    def first_block(slot, src_ref, r0, dst_ref, dmax_ref):
        for hh in range(ATT_HEADS_PER_STEP):
            q2_ref[slot, hh, :, 0:DA_VD] = _split_maps(src_ref[:, hh * DA_VD:(hh + 1) * DA_VD])
            q2_ref[slot, hh, :, DA_VD:2 * DA_VD] = jnp.broadcast_to(qbias_ref[hh], (2 * tq, LANES))
            score(r0, slot, hh, dst_ref, dmax_ref, causal=True)

    @pl.when(qi == 0)
    def _start_of_sequence():
        for hh in range(ATT_HEADS_PER_STEP):
            lanes = slice(hh * DA_VD, (hh + 1) * DA_VD)
            for c in range(seq // tq):
                vt_ref[hh, 0:DA_VD, c * tq:(c + 1) * tq] = v_ref[c * tq:(c + 1) * tq, lanes].T.astype(BF16)
            vt_ref[hh, DA_VD:dv1, :] = jnp.ones((ONES_ROWS, seq), BF16)
            vtm_ref[hh, 0:DA_VD, :] = _pad_rows(vm_ref[:, lanes], LANES).T.astype(BF16)
            vtm_ref[hh, DA_VD:dv1, :] = jnp.ones((ONES_ROWS, LANES), BF16)
        first_block(0, q_ref, 0, s0_ref, smax0_ref)

    def produce(j, slot):
        r0 = pl.multiple_of(j * tq, tq)
        for hh in range(ATT_HEADS_PER_STEP):
            score(r0, cur, hh, s_ref.at[slot], smax_ref.at[slot])

    def consume(j, slot):
        r0 = pl.multiple_of(j * tq, tq)
        for hh in range(ATT_HEADS_PER_STEP):
            m_prev = m_ref[hh]
            m_new = jnp.maximum(m_prev, smax_ref[slot, hh])
            p = jnp.exp2(s_ref[slot, hh] - m_new).astype(BF16)
            acc_ref[hh] = jnp.exp2(m_prev - m_new) * acc_ref[hh] + _dot(vt_ref[hh, :, pl.ds(r0, tq)], p)
            m_ref[hh] = m_new

    def next_first_block():
        first_block(1 - cur, qn_ref, pl.multiple_of(jnp.minimum(qi + 1, seq // tq - 1) * tq, tq), s0_ref, smax0_ref)

    r0 = pl.multiple_of(qi * tq, tq)
    n_meta = km_ref.shape[0]
    for hh in range(ATT_HEADS_PER_STEP):
        lanes = slice(hh * DA_VD, (hh + 1) * DA_VD)
        s_m = _dot_nt(jnp.concatenate([km_ref[:, lanes], poskm_ref[...]], axis=1), q2_ref[cur, hh])
        m_new = jnp.maximum(smax0_ref[hh], jnp.max(s_m, axis=0, keepdims=True))
        acc_ref[hh] = (_dot(vt_ref[hh, :, pl.ds(r0, tq)], jnp.exp2(s0_ref[hh] - m_new).astype(BF16))
                       + _dot(vtm_ref[hh, :, 0:n_meta], jnp.exp2(s_m - m_new).astype(BF16)))
        m_ref[hh] = m_new
    produce(jnp.maximum(qi - 1, 0), 0)

    @pl.when(qi == 0)
    def _only_block():
        next_first_block()

    def two_blocks(u, carry):
        j = qi - 1 - 2 * u
        consume(j, 0)
        produce(j - 1, 1)
        consume(j - 1, 1)
        produce(j - 2, 0)
        return carry

    n_back = jnp.maximum(qi - 1, 0)
    lax.fori_loop(0, n_back // 2, two_blocks, 0)

    @pl.when(n_back % 2 == 1)
    def _odd_block():
        consume(1, 0)
        produce(0, 1)

    @pl.when(jnp.logical_and(qi >= 1, n_back % 2 == 0))
    def _last_block_even():
        consume(0, 0)
        next_first_block()

    @pl.when(jnp.logical_and(qi >= 1, n_back % 2 == 1))
    def _last_block_odd():
        consume(0, 1)
        next_first_block()

    lam = _lambda(lamv_ref, lam_init)
    for hh in range(ATT_HEADS_PER_STEP):
        lanes = slice(hh * DA_VD, (hh + 1) * DA_VD)
        a = acc_ref[hh]
        on = a[0:DA_VD] * (1.0 / a[DA_VD:DA_VD + 1])
        o_t = on[:, :tq] - lam * on[:, tq:]
        o_ref[:, lanes] = (_rms(o_t.T, subln_ref[...], 1e-5) * (1.0 - lam_init)).astype(o_ref.dtype)


def _alibi_tables(seq_len, n_meta, tq):
    assert seq_len + n_meta <= POS_RADIX * 256
    slopes = jnp.exp2(-8.0 * jnp.arange(1, DA_HEADS + 1, dtype=F32) / DA_HEADS) * LOG2E
    hi, mid, lo = (t.astype(F32) for t in _split3(slopes))
    qbias = jnp.zeros((DA_HEADS, 1, LANES), F32)
    for i, t in enumerate((hi * POS_RADIX, mid * POS_RADIX, lo * POS_RADIX, hi, mid, lo)):
        qbias = qbias.at[:, 0, i].set(t)

    def digits(pos, n_rows):
        d = jnp.zeros((n_rows, LANES), F32)
        for i in range(3):
            d = d.at[:pos.shape[0], i].set((pos // POS_RADIX).astype(F32))
            d = d.at[:pos.shape[0], 3 + i].set((pos % POS_RADIX).astype(F32))
        return d.astype(BF16)

    posk = digits(n_meta + jnp.arange(seq_len), seq_len)
    poskm = digits(jnp.arange(n_meta), n_meta)
    key_t = jnp.arange(tq)[:, None]
    qry = jnp.arange(2 * tq)[None, :] % tq
    dmask = jnp.where(key_t <= qry, 0.0, NEG_INF).astype(F32)
    return posk, poskm, qbias.astype(BF16), dmask


def _attn_prompt(q, kb, v, kb_meta, v_meta, lamv, subln, n_seq, seq_len, n_meta, meta_row0, lam_init):
    tq = _row_tile(seq_len, 512)
    assert tq % LANES == 0
    nq = seq_len // tq
    hps = ATT_HEADS_PER_STEP
    wblk = hps * DA_VD
    meta_blk0 = meta_row0 // n_meta
    posk, poskm, qbias, dmask = _alibi_tables(seq_len, n_meta, tq)
    dv1 = DA_VD + ONES_ROWS
    body = functools.partial(_attn_prompt_body, tq=tq, lam_init=lam_init)
    const = lambda a: pl.BlockSpec(a.shape, lambda b, h, qi: (0,) * a.ndim, pipeline_mode=pl.Buffered(1))
    return pl.pallas_call(
        body,
        out_shape=jax.ShapeDtypeStruct((n_seq * seq_len, DA_HEADS * DA_VD), BF16),
        grid=(n_seq, DA_HEADS // hps, nq),
        in_specs=[pl.BlockSpec((tq, wblk), lambda b, h, qi: (b * nq + qi, h)),
                  pl.BlockSpec((tq, wblk), lambda b, h, qi: (b * nq + jnp.minimum(qi + 1, nq - 1), h)),
                  pl.BlockSpec((seq_len, wblk), lambda b, h, qi: (b, h)),
                  pl.BlockSpec((seq_len, wblk), lambda b, h, qi: (b, h)),
                  pl.BlockSpec((n_meta, wblk), lambda b, h, qi: (meta_blk0 + b, h)),
                  pl.BlockSpec((n_meta, wblk), lambda b, h, qi: (meta_blk0 + b, h)),
                  const(posk), const(poskm),
                  pl.BlockSpec((hps, 1, LANES), lambda b, h, qi: (h, 0, 0)),
                  const(dmask), const(lamv), const(subln)],
        out_specs=pl.BlockSpec((tq, wblk), lambda b, h, qi: (b * nq + qi, h)),
        scratch_shapes=[pltpu.VMEM((hps, dv1, seq_len), BF16), pltpu.VMEM((hps, dv1, LANES), BF16),
                        pltpu.VMEM((2, hps, 2 * tq, 2 * DA_VD), BF16), pltpu.VMEM((hps, 1, 2 * tq), F32),
                        pltpu.VMEM((hps, dv1, 2 * tq), F32), pltpu.VMEM((2, hps, tq, 2 * tq), F32),
                        pltpu.VMEM((hps, tq, 2 * tq), F32),
                        pltpu.VMEM((2, hps, 1, 2 * tq), F32), pltpu.VMEM((hps, 1, 2 * tq), F32)],
        compiler_params=_cparams(3),
        name="attn_prompt",
    )(q, q, kb, v, kb_meta, v_meta, posk, poskm, qbias, dmask, lamv, subln)


def _attn_meta_body(slopes_ref, q_ref, k_ref, v_ref, lamv_ref, subln_ref, o_ref,
                    m_ref, l_ref, acc_ref, *, n_meta, lam_init):
    h = pl.program_id(1)
    slope = slopes_ref[h]
    m_rows = 2 * n_meta
    m_ref[...] = jnp.full(m_ref.shape, NEG_INF, F32)
    l_ref[...] = jnp.zeros(l_ref.shape, F32)
    acc_ref[...] = jnp.zeros(acc_ref.shape, F32)
    qb = _split_maps(q_ref[...])
    r = lax.broadcasted_iota(jnp.int32, (m_rows, LANES), 0)
    r = jnp.where(r >= n_meta, r - n_meta, r)
    c = lax.broadcasted_iota(jnp.int32, (m_rows, LANES), 1)
    bias = jnp.where(c <= r, slope * c.astype(F32), NEG_INF)
    s = _dot_nt(qb, _pad_rows(k_ref[...], LANES).astype(BF16)) + bias
    _softmax_step([(s, _pad_rows(v_ref[...], LANES).astype(BF16))], m_ref, l_ref, acc_ref)
    on = acc_ref[...] / l_ref[...]
    o = on[:n_meta] - _lambda(lamv_ref, lam_init) * on[n_meta:]
    o_ref[...] = _rms(o, subln_ref[...], 1e-5) * (1.0 - lam_init)


def _attn_meta(q, k, v, slopes, lamv, subln, n_seq, n_meta, meta_row0, lam_init):
    blk0 = meta_row0 // n_meta
    spec = pl.BlockSpec((n_meta, DA_VD), lambda b, h: (blk0 + b, h))
    body = functools.partial(_attn_meta_body, n_meta=n_meta, lam_init=lam_init)
    return pl.pallas_call(
        body,
        out_shape=jax.ShapeDtypeStruct((n_seq * n_meta, DA_HEADS * DA_VD), F32),
        grid=(n_seq, DA_HEADS),
        in_specs=[pl.BlockSpec(memory_space=pltpu.SMEM), spec, spec, spec,
                  pl.BlockSpec(lamv.shape, lambda b, h: (0, 0)),
                  pl.BlockSpec(subln.shape, lambda b, h: (0, 0))],
        out_specs=pl.BlockSpec((n_meta, DA_VD), lambda b, h: (b, h)),
        scratch_shapes=[pltpu.VMEM((2 * n_meta, LANES), F32), pltpu.VMEM((2 * n_meta, LANES), F32),
                        pltpu.VMEM((2 * n_meta, DA_VD), F32)],
        compiler_params=_cparams(2),
        name="attn_meta",
    )(slopes, q, k, v, lamv, subln)


PAGES_PER_STEP = 8


def _attn_sample_body(pt_ref, q_ref, kn_ref, vn_ref, *rest, dec_len, n_pages, n_grp, lam_init):
    ck_refs, cv_refs = rest[:n_grp], rest[n_grp:2 * n_grp]
    (base_ref, basen_ref, rslope_ref, lamv_ref, subln_ref, o_ref,
     qb_ref, m_ref, l_ref, acc_ref, s_ref) = rest[2 * n_grp:]
    p = pl.program_id(1)
    rows_per_head = 2 * dec_len

    @pl.when(p == 0)
    def _start():
        lane = lax.broadcasted_iota(jnp.int32, (dec_len, DA_VD), 1)
        for h in range(DA_HEADS):
            qh = q_ref[:, h * DA_VD:(h + 1) * DA_VD]
            qb_ref[h * rows_per_head:h * rows_per_head + dec_len, :] = jnp.where(lane < DA_DH, qh, 0.0)
            qb_ref[h * rows_per_head + dec_len:(h + 1) * rows_per_head, :] = jnp.where(lane >= DA_DH, qh, 0.0)
        m_ref[...] = jnp.full(m_ref.shape, NEG_INF, F32)
        l_ref[...] = jnp.zeros(l_ref.shape, F32)
        acc_ref[...] = jnp.zeros(acc_ref.shape, F32)

    def head_rows(ref, h):
        return ref[0, pl.ds(h, PAGE, stride=DA_HEADS), :].astype(BF16)

    for u in range(n_grp):
        for h in range(DA_HEADS):
            hr = slice(h * rows_per_head, (h + 1) * rows_per_head)
            s_ref[hr, u * PAGE:(u + 1) * PAGE] = _dot_nt(qb_ref[hr, :], head_rows(ck_refs[u], h))
    s = s_ref[...] + base_ref[...]
    shift = rslope_ref[...] * ((p * n_grp - n_pages) * PAGE).astype(F32)
    m_prev = m_ref[...]
    m_new = jnp.maximum(m_prev, jnp.max(s, axis=1, keepdims=True) + shift)
    pm = jnp.exp2(s - (m_new - shift)[:, :1])
    alpha = jnp.exp2(m_prev - m_new)
    l_ref[...] = alpha * l_ref[...] + jnp.sum(pm, axis=1, keepdims=True)
    m_ref[...] = m_new
    pb = pm.astype(BF16)
    for h in range(DA_HEADS):
        hr = slice(h * rows_per_head, (h + 1) * rows_per_head)
        acc = alpha[hr] * acc_ref[hr, :]
        for u in range(n_grp):
            acc = acc + _dot(pb[hr, u * PAGE:(u + 1) * PAGE], head_rows(cv_refs[u], h))
        acc_ref[hr, :] = acc

    @pl.when(p == n_pages // n_grp - 1)
    def _finish():
        s = _dot_nt(qb_ref[...], _pad_rows(kn_ref[0], LANES).astype(BF16)) + basen_ref[...]
        _softmax_step([(s, _pad_rows(vn_ref[0], LANES).astype(BF16))], m_ref, l_ref, acc_ref)
        on = acc_ref[...] / l_ref[...]
        lam = _lambda(lamv_ref, lam_init)
        for h in range(DA_HEADS):
            o = (on[h * rows_per_head:h * rows_per_head + dec_len]
                 - lam * on[h * rows_per_head + dec_len:(h + 1) * rows_per_head])
            o_ref[:, h * DA_VD:(h + 1) * DA_VD] = _rms(o, subln_ref[...], 1e-5) * (1.0 - lam_init)


def _attn_sample(q, kn, vn, ck, cv, page_table, lamv, subln, n_seq, dec_len, samp_row0, lam_init):
    n_pages = page_table.shape[1]
    n_grp = math.gcd(n_pages, PAGES_PER_STEP)
    rows = DA_HEADS * 2 * dec_len
    assert rows == LANES and DA_HEADS * dec_len <= LANES
    page_rows = PAGE * DA_HEADS
    slopes = 2.0 ** (-8.0 * (jnp.arange(rows) // (2 * dec_len) + 1) / DA_HEADS) * LOG2E
    rhead = jnp.arange(rows) // (2 * dec_len)
    rq = jnp.arange(rows) % dec_len
    base = slopes[:, None] * jnp.arange(n_grp * PAGE)[None, :].astype(F32)
    cn = jnp.arange(LANES)
    valid = (cn[None, :] < DA_HEADS * dec_len) & (rhead[:, None] == (cn % DA_HEADS)[None, :]) \
        & ((cn // DA_HEADS)[None, :] <= rq[:, None])
    basen = jnp.where(valid, slopes[:, None] * (cn // DA_HEADS)[None, :].astype(F32), NEG_INF)
    rslope = jnp.broadcast_to(slopes[:, None], (rows, LANES)).astype(F32)
    blk0 = samp_row0 // dec_len
    body = functools.partial(_attn_sample_body, dec_len=dec_len, n_pages=n_pages, n_grp=n_grp, lam_init=lam_init)
    width = DA_HEADS * DA_VD

    def page(u):
        return pl.BlockSpec((1, page_rows, DA_VD), lambda i, p, pt: (pt[i, p * n_grp + u], 0, 0))

    const = lambda a: pl.BlockSpec(a.shape, lambda i, p, pt: (0,) * a.ndim)
    grid_spec = pltpu.PrefetchScalarGridSpec(
        num_scalar_prefetch=1,
        grid=(n_seq, n_pages // n_grp),
        in_specs=[pl.BlockSpec((dec_len, width), lambda i, p, pt: (blk0 + i, 0)),
                  pl.BlockSpec((1,) + kn.shape[1:], lambda i, p, pt: (i, 0, 0)),
                  pl.BlockSpec((1,) + vn.shape[1:], lambda i, p, pt: (i, 0, 0))]
                 + [page(u) for u in range(n_grp)] * 2
                 + [const(base), const(basen), const(rslope), const(lamv), const(subln)],
        out_specs=pl.BlockSpec((dec_len, width), lambda i, p, pt: (i, 0)),
        scratch_shapes=[pltpu.VMEM((rows, DA_VD), BF16), pltpu.VMEM((rows, LANES), F32),
                        pltpu.VMEM((rows, LANES), F32), pltpu.VMEM((rows, DA_VD), F32),
                        pltpu.VMEM((rows, n_grp * PAGE), F32)])
    return pl.pallas_call(
        body,
        out_shape=jax.ShapeDtypeStruct((n_seq * dec_len, width), F32),
        grid_spec=grid_spec,
        compiler_params=_cparams(2),
        name="attn_sample",
    )(page_table, q, kn, vn, *([ck] * n_grp), *([cv] * n_grp), base.astype(F32), basen.astype(F32), rslope,
      lamv, subln)


def kernel(x_prompt, x_sample, state_gla, state_ssd, state_conv, cache_k, cache_v, page_table, meta_tokens, norm_ffn1, w_ffn1_gu, w_ffn1_down, norm_mix, norm_ffn2, w_ffn2_gu, w_ffn2_down, w_in_ab, gla_gk_w2, gla_gk_b, gla_norm, ssd_conv_w, ssd_conv_b, ssd_dt_bias, ssd_a_log, ssd_d, ssd_norm, w_out_ab, w_qkv_c, lam_q1, lam_k1, lam_q2, lam_k2, subln_c, w_out_c, norm_final):
    n_b, seq, d = x_prompt.shape
    n_s, dec, _ = x_sample.shape
    n_meta = meta_tokens.shape[0]
    depth = norm_ffn1.shape[0]
    assert seq % CHUNK == 0 and n_meta % SUBLANES == 0 and dec % SUBLANES == 0
    main_rows, meta_rows, samp_rows = n_b * seq, n_b * n_meta, n_s * dec
    meta_row0, samp_row0 = main_rows, main_rows + meta_rows

    x = jnp.concatenate([x_prompt.reshape(main_rows, d),
                         jnp.tile(meta_tokens.astype(F32), (n_b, 1)),
                         x_sample.reshape(samp_rows, d)], axis=0)

    row = lambda a: a.reshape(1, -1).astype(F32)
    gla_p, gla_s, ssd_p, ssd_s, conv_p, conv_s = [], [], [], [], [], []
    k_p, v_p, k_s, v_s = [], [], [], []
    conv_dim = ssd_conv_w.shape[2]
    gqk, gv = GLA_HEADS * GLA_DK, GLA_HEADS * GLA_DV
    bounds = [0]
    for w in (gqk, gqk, gv, GLA_RANK, gv, SSD_INNER, conv_dim, SSD_HEADS):
        bounds.append(bounds[-1] + w)
    expand = (jnp.arange(SSD_INNER)[None, :] // SSD_HEADDIM == jnp.arange(SSD_HEADS)[:, None]).astype(BF16)

    for l in range(depth):
        x = _ffn(x, row(norm_ffn1[l]), *_ffn_weights(w_ffn1_gu[l], w_ffn1_down[l]), row(norm_final), False)
        if l % 2 == 0:
            e = l // 2
            ws = [w_in_ab[e][:, bounds[i]:bounds[i + 1]].astype(BF16) for i in range(8)]
            proj = _inproj_ab(x, row(norm_mix[l]), ws, gla_gk_w2[e].astype(BF16), row(gla_gk_b[e]),
                              row(ssd_dt_bias[e]))
            params = (row(gla_norm[e]), ssd_conv_w[e].astype(F32), row(ssd_conv_b[e]), row(ssd_a_log[e]),
                      row(jnp.repeat(ssd_d[e], SSD_HEADDIM)), row(ssd_norm[e]), expand)
            ssd_shape = (SSD_INNER, SSD_STATE)
            zg = jnp.zeros((n_b, GLA_HEADS, GLA_DK, GLA_DV), F32)
            zs = jnp.zeros((n_b,) + ssd_shape, F32)
            zc = jnp.zeros((n_b, SSD_CONV - 1, conv_dim), F32)
            mix_meta, sg, ss, cv = _scan(proj, meta_row0, n_b, n_meta, n_meta, zg, zs, zc, params)
            mix_main, sg, ss, cv = _scan(proj, 0, n_b, seq, CHUNK, sg, ss, cv, params)
            gla_p.append(sg)
            ssd_p.append(ss.reshape(n_b, SSD_HEADS, SSD_HEADDIM, SSD_STATE))
            conv_p.append(cv)
            mix_samp, sg, ss, cv = _scan(proj, samp_row0, n_s, dec, dec, state_gla[e].astype(F32),
                                         state_ssd[e].astype(F32).reshape((n_s,) + ssd_shape),
                                         state_conv[e].astype(F32), params)
            gla_s.append(sg)
            ssd_s.append(ss.reshape(n_s, SSD_HEADS, SSD_HEADDIM, SSD_STATE))
            conv_s.append(cv)
            w_out = w_out_ab[e].astype(BF16)
            for mix, row0 in ((mix_main, 0), (mix_meta, meta_row0), (mix_samp, samp_row0)):
                x = _outproj(x, mix, w_out, row0)
        else:
            o = l // 2
            lam_init = 0.8 - 0.6 * math.exp(-0.3 * l)
            width = DA_HEADS * DA_VD
            wq, wk, wv = (w_qkv_c[o][:, i * width:(i + 1) * width].astype(BF16) for i in range(3))
            q, k, v, kb = _qkv(x, row(norm_mix[l]), wq, wk, wv, 0, main_rows)
            q_r, k_r, v_r, kb_r = _qkv(x, row(norm_mix[l]), wq, wk, wv, main_rows, meta_rows + samp_rows)
            slopes = jnp.exp2(-8.0 * jnp.arange(1, DA_HEADS + 1, dtype=F32) / DA_HEADS) * LOG2E
            lamv = jnp.stack([lam_q1[o], lam_k1[o], lam_q2[o], lam_k2[o]]).astype(F32)
            subln = row(subln_c[o])
            o_main = _attn_prompt(q, kb, v, kb_r, v_r, lamv, subln, n_b, seq, n_meta, 0, lam_init)
            o_meta = _attn_meta(q_r, k_r, v_r, slopes, lamv, subln, n_b, n_meta, 0, lam_init)
            kn = k_r[meta_rows:].reshape(n_s, dec * DA_HEADS, DA_VD)
            vn = v_r[meta_rows:].reshape(n_s, dec * DA_HEADS, DA_VD)
            ck = cache_k[o].reshape(cache_k.shape[1], PAGE * DA_HEADS, DA_VD)
            cv_ = cache_v[o].reshape(cache_v.shape[1], PAGE * DA_HEADS, DA_VD)
            o_samp = _attn_sample(q_r, kn, vn, ck, cv_, page_table, lamv, subln, n_s, dec, meta_rows, lam_init)
            w_out = w_out_c[o].astype(BF16)
            for att, row0 in ((o_main, 0), (o_meta, meta_row0), (o_samp, samp_row0)):
                x = _outproj(x, att, w_out, row0)
            rows5 = lambda a, n, t: a.reshape(n, t, DA_HEADS, DA_VD)
            k_p.append(jnp.concatenate([rows5(k_r[:meta_rows], n_b, n_meta), rows5(k, n_b, seq)], axis=1))
            v_p.append(jnp.concatenate([rows5(v_r[:meta_rows], n_b, n_meta), rows5(v, n_b, seq)], axis=1))
            k_s.append(rows5(k_r[meta_rows:], n_s, dec))
            v_s.append(rows5(v_r[meta_rows:], n_s, dec))
        ffn2 = (row(norm_ffn2[l]), *_ffn_weights(w_ffn2_gu[l], w_ffn2_down[l]), row(norm_final))
        if l < depth - 1:
            x = _ffn(x, *ffn2, False)

    y_main = _ffn(x, *ffn2, True, 0, main_rows, 1024)
    y_rest = _ffn(x, *ffn2, True, main_rows, meta_rows + samp_rows, 1024)
    y_prompt = y_main.reshape(n_b, seq, d)
    y_sample = y_rest[meta_rows:].reshape(n_s, dec, d)
    return (y_prompt, y_sample, jnp.stack(gla_p), jnp.stack(gla_s), jnp.stack(ssd_p), jnp.stack(ssd_s),
            jnp.stack(conv_p), jnp.stack(conv_s), jnp.stack(k_p), jnp.stack(v_p), jnp.stack(k_s), jnp.stack(v_s))
```
